```python
import math
import jax, jax.numpy as jnp
from jax import lax
import numpy as np

D_MODEL = 4096
BATCH = 2
SEQ = 8192
DEPTH = 2

GRID_W = 64
CTX_LEN = 256
N_MIXERS = 4
MIX_W = D_MODEL
GROUP_W = MIX_W // N_MIXERS
FNET_HEADS = 4
FNET_HEAD_DIM = GROUP_W // FNET_HEADS
S5_GROUP_CH = 16
S5_GROUPS = GROUP_W // S5_GROUP_CH
S5_STATE = 64
CONV_WIDTH = 31
SSD_HEAD_DIM = 64
SSD_HEADS = GROUP_W // SSD_HEAD_DIM
SSD_NGROUPS = 4
SSD_STATE = 128
SSD_CONV = 5
SSD_CHUNK = 128
SSD_XBC = GROUP_W + 2 * SSD_NGROUPS * SSD_STATE
SSD_IN = GROUP_W + SSD_XBC + 2 * SSD_HEADS
OFF_A = 0
OFF_S5 = GROUP_W
OFF_CV = 2 * GROUP_W
OFF_SSD = 4 * GROUP_W
IN_COLS = OFF_SSD + SSD_IN
FFN_HIDDEN = ((8 * D_MODEL + 2) // 3 + 255) // 256 * 256
ALPHA = (2 * DEPTH) ** 0.25
BETA = (8 * DEPTH) ** -0.25
LN_EPS = 1e-5
RMS_EPS = 1e-5

kernel_name = "hybrid_fnet_s5_conformer_ssd_dit"

F32 = jnp.float32


def layer_norm(x, g, b):
    xf = x.astype(F32)
    mu = jnp.mean(xf, -1, keepdims=True)
    var = jnp.mean(jnp.square(xf - mu), -1, keepdims=True)
    return ((xf - mu) * lax.rsqrt(var + LN_EPS) * g + b).astype(x.dtype)


def pos_2d(rows, dim):
    t = jnp.arange(rows * GRID_W)
    r = (t // GRID_W).astype(F32)[:, None]
    col = (t % GRID_W).astype(F32)[:, None]
    q = dim // 4
    omega = jnp.exp(-math.log(10000.0) * jnp.arange(q, dtype=F32) / q)[None]
    return jnp.concatenate([jnp.sin(r * omega), jnp.cos(r * omega),
                            jnp.sin(col * omega), jnp.cos(col * omega)], -1)


def depthwise_conv(x, w, bias):
    k = w.shape[0]
    pad = (k - 1) // 2
    y = lax.conv_general_dilated(x, w[:, None, :].astype(x.dtype), (1,), [(pad, k - 1 - pad)],
                                 dimension_numbers=("NWC", "WIO", "NWC"),
                                 feature_group_count=x.shape[-1])
    return y + bias.astype(x.dtype)


def fourier_mix(ua):
    b, l, _ = ua.shape
    u = ua.astype(F32).reshape(b, l, FNET_HEADS, FNET_HEAD_DIM)
    y = jnp.fft.fft2(u, axes=(1, 3), norm="ortho").real
    return y.reshape(b, l, GROUP_W)


def _lin_combine(e1, e2):
    a1, b1 = e1
    a2, b2 = e2
    return a1 * a2, a2 * b1 + b2


def s5_mix(us, p, h0_f, h0_b):
    b, l, _ = us.shape
    u = us.astype(F32).reshape(b, l, S5_GROUPS, S5_GROUP_CH)
    bmat = lax.complex(p["s5_b_re"].astype(F32), p["s5_b_im"].astype(F32))
    cmat = lax.complex(p["s5_c_re"].astype(F32), p["s5_c_im"].astype(F32))
    bu = jnp.einsum("blgc,gpc->blgp", u, bmat)

    def direction(k, h0, reverse):
        lam = lax.complex(p["s5_lam_re"][k].astype(F32), p["s5_lam_im"][k].astype(F32))
        dt = jnp.exp(p["s5_log_dt"][k].astype(F32))[:, None]
        a_bar = jnp.exp(lam * dt)
        inp = bu * ((a_bar - 1.0) / lam)
        edge = -1 if reverse else 0
        inp = inp.at[:, edge].add(a_bar * h0)
        a = jnp.broadcast_to(a_bar, inp.shape)
        _, h = lax.associative_scan(_lin_combine, (a, inp), axis=1, reverse=reverse)
        return h

    h_f = direction(0, h0_f, False)
    h_b = direction(1, h0_b, True)
    y = jnp.einsum("blgp,gcp->blgc", h_f + h_b, cmat).real
    y = y + p["s5_d"].astype(F32).reshape(S5_GROUPS, S5_GROUP_CH) * u
    g = jax.nn.gelu(y.reshape(b, l, GROUP_W))
    out = g * jax.nn.sigmoid(g @ p["s5_w_glu"].astype(F32))
    return out, h_f[:, -1], h_b[:, 0]


def conv_module(uc, p):
    v, gate = jnp.split(uc.astype(F32), 2, axis=-1)
    h = v * jax.nn.sigmoid(gate)
    h = depthwise_conv(h, p["cv_w"].astype(F32), p["cv_b"].astype(F32))
    h = layer_norm(h, p["cv_ln_g"], p["cv_ln_b"])
    return jax.nn.silu(h)


def ssd_chunked(xs, dt, a_head, bm, cm, h0):
    b, l, nh, hp = xs.shape
    g, n = bm.shape[2], bm.shape[3]
    e = nh // g
    q = SSD_CHUNK
    c = l // q
    x = xs.reshape(b, c, q, g, e, hp)
    dt = dt.reshape(b, c, q, g, e)
    bc = bm.reshape(b, c, q, g, n)
    cc = cm.reshape(b, c, q, g, n)
    a_cum = jnp.cumsum(dt * a_head.reshape(g, e), axis=2)
    seg = a_cum[:, :, :, None] - a_cum[:, :, None]
    mask = jnp.tril(jnp.ones((q, q), bool))[:, :, None, None]
    decay = jnp.where(mask, jnp.exp(jnp.where(mask, seg, 0.0)), 0.0)
    scores = jnp.einsum("bcign,bcjgn->bcijg", cc, bc)
    w = scores[..., None] * decay * dt[:, :, None]
    y_diag = jnp.einsum("bcijge,bcjgep->bcigep", w, x)
    a_last = a_cum[:, :, -1]
    decay_end = jnp.exp(a_last[:, :, None] - a_cum)
    states = jnp.einsum("bcjgn,bcjge,bcjgep->bcgepn", bc, decay_end * dt, x)

    def step(s, inp):
        dec, st = inp
        return dec[..., None, None] * s + st, s

    final, prev = lax.scan(step, h0, (jnp.moveaxis(jnp.exp(a_last), 1, 0),
                                      jnp.moveaxis(states, 1, 0)))
    prev = jnp.moveaxis(prev, 0, 1)
    y_off = jnp.einsum("bcign,bcgepn,bcige->bcigep", cc, prev, jnp.exp(a_cum))
    return (y_diag + y_off).reshape(b, l, nh, hp), final


def ssd_mix(ud, p, h0_f, h0_b):
    b, l, _ = ud.shape
    ud = ud.astype(F32)
    z = ud[..., :GROUP_W]
    xbc = jax.nn.silu(depthwise_conv(ud[..., GROUP_W:GROUP_W + SSD_XBC],
                                     p["ssd_conv_w"].astype(F32), p["ssd_conv_b"].astype(F32)))
    dt_raw = ud[..., GROUP_W + SSD_XBC:].reshape(b, l, 2, SSD_HEADS)
    nb = SSD_NGROUPS * SSD_STATE
    xs = xbc[..., :GROUP_W].reshape(b, l, SSD_HEADS, SSD_HEAD_DIM)
    bm = xbc[..., GROUP_W:GROUP_W + nb].reshape(b, l, SSD_NGROUPS, SSD_STATE)
    cm = xbc[..., GROUP_W + nb:].reshape(b, l, SSD_NGROUPS, SSD_STATE)
    dt = jax.nn.softplus(dt_raw + p["ssd_dt_bias"].astype(F32))
    a_head = -jnp.exp(p["ssd_a_log"].astype(F32))
    y_f, s_f = ssd_chunked(xs, dt[:, :, 0], a_head[0], bm, cm, h0_f)
    fl = lambda t: jnp.flip(t, axis=1)
    y_b, s_b = ssd_chunked(fl(xs), fl(dt[:, :, 1]), a_head[1], fl(bm), fl(cm), h0_b)
    y = y_f + fl(y_b) + p["ssd_d"].astype(F32)[:, None] * xs
    t = (y.reshape(b, l, GROUP_W) * jax.nn.silu(z)).reshape(b, l, SSD_NGROUPS, -1)
    t = t * lax.rsqrt(jnp.mean(jnp.square(t), -1, keepdims=True) + RMS_EPS)
    return t.reshape(b, l, GROUP_W) * p["ssd_norm_w"].astype(F32), s_f, s_b


def token_mixing(h, hc, p, with_ctx):
    u = h @ p["w_in"]
    uc = hc @ p["w_in"]
    bc = hc.shape[0]
    z_s5 = jnp.zeros((bc, S5_GROUPS, S5_STATE), jnp.complex64)
    z_ssd = jnp.zeros((bc, SSD_NGROUPS, SSD_HEADS // SSD_NGROUPS, SSD_HEAD_DIM, SSD_STATE), F32)
    ys_c, s5_f, s5_b = s5_mix(uc[..., OFF_S5:OFF_S5 + GROUP_W], p, z_s5, z_s5)
    yd_c, sd_f, sd_b = ssd_mix(uc[..., OFF_SSD:], p, z_ssd, z_ssd)
    ys, _, _ = s5_mix(u[..., OFF_S5:OFF_S5 + GROUP_W], p, s5_f, s5_b)
    yd, _, _ = ssd_mix(u[..., OFF_SSD:], p, sd_f, sd_b)
    ya = fourier_mix(u[..., OFF_A:OFF_A + GROUP_W])
    yc = conv_module(u[..., OFF_CV:OFF_SSD], p)
    out = jnp.concatenate([ya, ys, yc, yd], -1).astype(h.dtype) @ p["w_out"]
    if not with_ctx:
        return out, None
    ya_c = fourier_mix(uc[..., OFF_A:OFF_A + GROUP_W])
    yc_c = conv_module(uc[..., OFF_CV:OFF_SSD], p)
    out_c = jnp.concatenate([ya_c, ys_c, yc_c, yd_c], -1).astype(hc.dtype) @ p["w_out"]
    return out, out_c


def swiglu(h, p):
    return (jax.nn.silu(h @ p["w_gate"]) * (h @ p["w_up"])) @ p["w_down"]


def setup_inputs(seed: int = 0) -> dict:
    key = jax.random.key(seed)
    ks = iter(jax.random.split(key, 48))
    L, D, G = DEPTH, D_MODEL, GROUP_W

    def nrm(shape, scale):
        return jax.random.normal(next(ks), shape, F32) * scale

    def uni(shape, lo, hi):
        return jax.random.uniform(next(ks), shape, F32, lo, hi)

    n_idx = jnp.arange(S5_STATE, dtype=F32)
    dt0 = jnp.exp(uni((L, 2, SSD_HEADS), math.log(1e-3), math.log(1e-1)))
    return {
        "x": nrm((BATCH, SEQ, D), 1.0),
        "c": nrm((BATCH, D), 1.0),
        "ctx": nrm((BATCH, CTX_LEN, D), 1.0),
        "c_ctx": nrm((D,), 1.0),
        "w_ada": nrm((L, D, 6 * D), 0.5 * D ** -0.5),
        "b_ada": nrm((L, 6 * D), 0.02),
        "w_in": nrm((L, D, IN_COLS), D ** -0.5),
        "s5_lam_re": -0.5 + nrm((L, 2, S5_GROUPS, S5_STATE), 0.01),
        "s5_lam_im": math.pi * n_idx + nrm((L, 2, S5_GROUPS, S5_STATE), 0.01),
        "s5_log_dt": uni((L, 2, S5_GROUPS), math.log(1e-3), math.log(1e-1)),
        "s5_b_re": nrm((L, S5_GROUPS, S5_STATE, S5_GROUP_CH), (2 * S5_GROUP_CH) ** -0.5),
        "s5_b_im": nrm((L, S5_GROUPS, S5_STATE, S5_GROUP_CH), (2 * S5_GROUP_CH) ** -0.5),
        "s5_c_re": nrm((L, S5_GROUPS, S5_GROUP_CH, S5_STATE), S5_STATE ** -0.5),
        "s5_c_im": nrm((L, S5_GROUPS, S5_GROUP_CH, S5_STATE), S5_STATE ** -0.5),
        "s5_d": nrm((L, G), 1.0),
        "s5_w_glu": nrm((L, G, G), G ** -0.5),
        "cv_w": nrm((L, CONV_WIDTH, G), CONV_WIDTH ** -0.5),
        "cv_b": nrm((L, G), 0.02),
        "cv_ln_g": 1.0 + nrm((L, G), 0.02),
        "cv_ln_b": nrm((L, G), 0.02),
        "ssd_conv_w": nrm((L, SSD_CONV, SSD_XBC), SSD_CONV ** -0.5),
        "ssd_conv_b": nrm((L, SSD_XBC), 0.02),
        "ssd_a_log": jnp.log(uni((L, 2, SSD_HEADS), 1.0, 16.0)),
        "ssd_dt_bias": dt0 + jnp.log(-jnp.expm1(-dt0)),
        "ssd_d": 1.0 + nrm((L, SSD_HEADS), 0.1),
        "ssd_norm_w": 1.0 + nrm((L, G), 0.02),
        "w_out": nrm((L, MIX_W, D), BETA * MIX_W ** -0.5),
        "ln1_g": 1.0 + nrm((L, D), 0.02),
        "ln1_b": nrm((L, D), 0.02),
        "w_gate": nrm((L, D, FFN_HIDDEN), D ** -0.5),
        "w_up": nrm((L, D, FFN_HIDDEN), D ** -0.5),
        "w_down": nrm((L, FFN_HIDDEN, D), BETA * FFN_HIDDEN ** -0.5),
        "ln2_g": 1.0 + nrm((L, D), 0.02),
        "ln2_b": nrm((L, D), 0.02),
    }


def reference(x, c, ctx, c_ctx, w_ada, b_ada, w_in, s5_lam_re, s5_lam_im, s5_log_dt,
              s5_b_re, s5_b_im, s5_c_re, s5_c_im, s5_d, s5_w_glu, cv_w, cv_b, cv_ln_g,
              cv_ln_b, ssd_conv_w, ssd_conv_b, ssd_a_log, ssd_dt_bias, ssd_d, ssd_norm_w,
              w_out, ln1_g, ln1_b, w_gate, w_up, w_down, ln2_g, ln2_b):
    bsz, seq, dm = x.shape
    rows = seq // GRID_W
    h = x + pos_2d(rows, dm).astype(x.dtype)[None]
    hc = ctx
    for i in range(DEPTH):
        with_ctx = i < DEPTH - 1
        p = {"w_in": w_in[i], "w_out": w_out[i],
             "s5_lam_re": s5_lam_re[i], "s5_lam_im": s5_lam_im[i], "s5_log_dt": s5_log_dt[i],
             "s5_b_re": s5_b_re[i], "s5_b_im": s5_b_im[i], "s5_c_re": s5_c_re[i],
             "s5_c_im": s5_c_im[i], "s5_d": s5_d[i], "s5_w_glu": s5_w_glu[i],
             "cv_w": cv_w[i], "cv_b": cv_b[i], "cv_ln_g": cv_ln_g[i], "cv_ln_b": cv_ln_b[i],
             "ssd_conv_w": ssd_conv_w[i], "ssd_conv_b": ssd_conv_b[i], "ssd_a_log": ssd_a_log[i],
             "ssd_dt_bias": ssd_dt_bias[i], "ssd_d": ssd_d[i], "ssd_norm_w": ssd_norm_w[i],
             "w_gate": w_gate[i], "w_up": w_up[i], "w_down": w_down[i]}
        m = jax.nn.silu(c) @ w_ada[i] + b_ada[i]
        sh1, sc1, g1, sh2, sc2, g2 = jnp.split(m[:, None, :], 6, axis=-1)
        mc = jax.nn.silu(c_ctx) @ w_ada[i] + b_ada[i]
        csh1, csc1, cg1, csh2, csc2, cg2 = jnp.split(mc, 6)
        mix, mix_c = token_mixing(h * (1.0 + sc1) + sh1, hc * (1.0 + csc1) + csh1, p, with_ctx)
        h = layer_norm(ALPHA * h + g1 * mix, ln1_g[i], ln1_b[i])
        h = layer_norm(ALPHA * h + g2 * swiglu(h * (1.0 + sc2) + sh2, p), ln2_g[i], ln2_b[i])
        if with_ctx:
            hc = layer_norm(ALPHA * hc + cg1 * mix_c, ln1_g[i], ln1_b[i])
            hc = layer_norm(ALPHA * hc + cg2 * swiglu(hc * (1.0 + csc2) + csh2, p), ln2_g[i], ln2_b[i])
    return h.astype(x.dtype)
```

```python
import functools
import math

import jax
import jax.numpy as jnp
import numpy as np
from jax import lax
from jax.experimental import pallas as pl
from jax.experimental.pallas import tpu as pltpu

F32 = jnp.float32
BF16 = jnp.bfloat16
HIGHEST = lax.Precision.HIGHEST

GRID_W = 64
N_MIXERS = 4
FNET_HEADS = 4
S5_GROUP_CH = 16
S5_STATE = 64
CONV_WIDTH = 31
SSD_HEAD_DIM = 64
SSD_NGROUPS = 4
SSD_STATE = 128
SSD_CONV = 5
LN_EPS = 1e-5
RMS_EPS = 1e-5

V7X_LANES = 128
V7X_SUBLANES = 8
V7X_VMEM_LIMIT_BYTES = 60000 * 1024


def _cparams(n_axes, vmem_bytes):
    limit = int(min(max(vmem_bytes * 5 // 4 + (2 << 20), 16 << 20), V7X_VMEM_LIMIT_BYTES))
    return pltpu.CompilerParams(dimension_semantics=("arbitrary",) * n_axes,
                                vmem_limit_bytes=limit)


def _silu(x):
    return x * jax.nn.sigmoid(x)


def _gelu_tanh(x):
    return 0.5 * x * (1.0 + jnp.tanh(math.sqrt(2.0 / math.pi) * (x + 0.044715 * (x * x * x))))


def _softplus(x):
    return jnp.maximum(x, 0.0) + jnp.log1p(jnp.exp(-jnp.abs(x)))


def _ada_kernel(c_ref, w_ref, b_ref, o_ref):
    cs = _silu(c_ref[...])
    o_ref[0] = jnp.dot(cs, w_ref[0], precision=HIGHEST, preferred_element_type=F32) + b_ref[0]


def ada_modulation(c8, w_ada, b_ada, tn=1024):
    depth, d, n = w_ada.shape
    tn = min(tn, n)
    return pl.pallas_call(
        _ada_kernel,
        grid=(depth, n // tn),
        in_specs=[pl.BlockSpec((V7X_SUBLANES, d), lambda l, j: (0, 0)),
                  pl.BlockSpec((1, d, tn), lambda l, j: (l, 0, j)),
                  pl.BlockSpec((1, 1, tn), lambda l, j: (l, 0, j))],
        out_specs=pl.BlockSpec((1, V7X_SUBLANES, tn), lambda l, j: (l, 0, j)),
        out_shape=jax.ShapeDtypeStruct((depth, V7X_SUBLANES, n), F32),
        compiler_params=_cparams(2, 2 * d * tn * 4 + 6 * d * tn * 2),
        name="ada_modulation",
    )(c8, w_ada, b_ada.reshape(depth, 1, n))


def _prologue_kernel(x_ref, tr_ref, tc_ref, sc_ref, sh_ref, h_ref, hm_ref, *, tp, half):
    sc = 1.0 + sc_ref[0]
    sh = sh_ref[0]
    for rr in range(tp // GRID_W):
        rows = slice(rr * GRID_W, (rr + 1) * GRID_W)
        lo = x_ref[0, rows, :half] + tr_ref[rr:rr + 1, :]
        hi = x_ref[0, rows, half:] + tc_ref[...]
        h_ref[0, rows, :half] = lo
        h_ref[0, rows, half:] = hi
        hm_ref[0, rows, :half] = (lo * sc[:, :half] + sh[:, :half]).astype(BF16)
        hm_ref[0, rows, half:] = (hi * sc[:, half:] + sh[:, half:]).astype(BF16)


def prologue(x, sc, sh, tp=512):
    b, l, d = x.shape
    tp = min(tp, l)
    half = d // 2
    q = d // 4
    omega = jnp.exp(-math.log(10000.0) * jnp.arange(q, dtype=F32) / q)[None]
    r = jnp.arange(l // GRID_W, dtype=F32)[:, None]
    col = jnp.arange(GRID_W, dtype=F32)[:, None]
    tab_r = jnp.concatenate([jnp.sin(r * omega), jnp.cos(r * omega)], -1)
    tab_c = jnp.concatenate([jnp.sin(col * omega), jnp.cos(col * omega)], -1)
    rpt = tp // GRID_W
    return pl.pallas_call(
        functools.partial(_prologue_kernel, tp=tp, half=half),
        grid=(b, l // tp),
        in_specs=[pl.BlockSpec((1, tp, d), lambda bb, i: (bb, i, 0)),
                  pl.BlockSpec((rpt, half), lambda bb, i: (i, 0)),
                  pl.BlockSpec((GRID_W, half), lambda bb, i: (0, 0)),
                  pl.BlockSpec((1, 1, d), lambda bb, i: (bb, 0, 0)),
                  pl.BlockSpec((1, 1, d), lambda bb, i: (bb, 0, 0))],
        out_specs=[pl.BlockSpec((1, tp, d), lambda bb, i: (bb, i, 0)),
                   pl.BlockSpec((1, tp, d), lambda bb, i: (bb, i, 0))],
        out_shape=[jax.ShapeDtypeStruct((b, l, d), F32), jax.ShapeDtypeStruct((b, l, d), BF16)],
        compiler_params=_cparams(2, 2 * tp * d * (4 + 4 + 2)),
        name="prologue",
    )(x, tab_r, tab_c, sc, sh)


def _modulate_kernel(x_ref, sc_ref, sh_ref, o_ref):
    o_ref[0] = (x_ref[0] * (1.0 + sc_ref[0]) + sh_ref[0]).astype(BF16)


def modulate(x, sc, sh, tp=256):
    b, l, d = x.shape
    tp = min(tp, l)
    return pl.pallas_call(
        _modulate_kernel,
        grid=(b, l // tp),
        in_specs=[pl.BlockSpec((1, tp, d), lambda bb, i: (bb, i, 0)),
                  pl.BlockSpec((1, 1, d), lambda bb, i: (bb, 0, 0)),
                  pl.BlockSpec((1, 1, d), lambda bb, i: (bb, 0, 0))],
        out_specs=pl.BlockSpec((1, tp, d), lambda bb, i: (bb, i, 0)),
        out_shape=jax.ShapeDtypeStruct((b, l, d), BF16),
        compiler_params=_cparams(2, 2 * tp * d * 6),
        name="modulate",
    )(x, sc, sh)


def _mm_kernel(x_ref, w_ref, o_ref):
    o_ref[...] = jnp.dot(x_ref[...], w_ref[...], preferred_element_type=F32).astype(o_ref.dtype)


def matmul(x, w, *, tm=1024, tn=1024, out_dtype=F32, name="matmul"):
    m, k = x.shape
    n = w.shape[1]
    tm, tn = min(tm, m), min(tn, n)
    ob = jnp.dtype(out_dtype).itemsize
    return pl.pallas_call(
        _mm_kernel,
        grid=(n // tn, m // tm),
        in_specs=[pl.BlockSpec((tm, k), lambda j, i: (i, 0)),
                  pl.BlockSpec((k, tn), lambda j, i: (0, j))],
        out_specs=pl.BlockSpec((tm, tn), lambda j, i: (i, j)),
        out_shape=jax.ShapeDtypeStruct((m, n), out_dtype),
        compiler_params=_cparams(2, 2 * (tm * k * 2 + k * tn * 2 + tm * tn * ob) + tm * tn * 4),
        name=name,
    )(x, w)


def _gate_up_kernel(x_ref, wg_ref, wu_ref, o_ref):
    x = x_ref[...]
    g = jnp.dot(x, wg_ref[...], preferred_element_type=F32)
    u = jnp.dot(x, wu_ref[...], preferred_element_type=F32)
    o_ref[...] = (_silu(g) * u).astype(o_ref.dtype)


def gate_up(x, wg, wu, *, tm=1024, tn=512):
    m, k = x.shape
    n = wg.shape[1]
    tm, tn = min(tm, m), min(tn, n)
    return pl.pallas_call(
        _gate_up_kernel,
        grid=(n // tn, m // tm),
        in_specs=[pl.BlockSpec((tm, k), lambda j, i: (i, 0)),
                  pl.BlockSpec((k, tn), lambda j, i: (0, j)),
                  pl.BlockSpec((k, tn), lambda j, i: (0, j))],
        out_specs=pl.BlockSpec((tm, tn), lambda j, i: (i, j)),
        out_shape=jax.ShapeDtypeStruct((m, n), BF16),
        compiler_params=_cparams(2, 2 * (tm * k * 2 + 2 * k * tn * 2 + tm * tn * 2) + 3 * tm * tn * 4),
        name="gate_up",
    )(x, wg, wu)


def _mm_ktiled_kernel(*refs, n_lhs):
    x_refs, w_ref, o_ref = refs[:n_lhs], refs[n_lhs], refs[n_lhs + 1]
    k = pl.program_id(1)

    @pl.when(k == 0)
    def _():
        o_ref[...] = jnp.zeros_like(o_ref)

    if n_lhs == 1:
        o_ref[...] += jnp.dot(x_refs[0][...], w_ref[...], preferred_element_type=F32)
    else:
        for idx in range(n_lhs):
            @pl.when(k == idx)
            def _(idx=idx):
                o_ref[...] += jnp.dot(x_refs[idx][...], w_ref[...], preferred_element_type=F32)


def matmul_ktiled(xs, w, *, tm=512, tk=1024, name="matmul_ktiled"):
    if isinstance(xs, (list, tuple)):
        m, tk = xs[0].shape
        n_lhs = len(xs)
        tm = min(tm, m)
        x_specs = [pl.BlockSpec((tm, tk), lambda i, k: (i, 0)) for _ in xs]
        xs = list(xs)
    else:
        m = xs.shape[0]
        n_lhs = 1
        tm, tk = min(tm, m), min(tk, xs.shape[1])
        x_specs = [pl.BlockSpec((tm, tk), lambda i, k: (i, k))]
        xs = [xs]
    kk, n = w.shape
    return pl.pallas_call(
        functools.partial(_mm_ktiled_kernel, n_lhs=n_lhs),
        grid=(m // tm, kk // tk),
        in_specs=x_specs + [pl.BlockSpec((tk, n), lambda i, k: (k, 0))],
        out_specs=pl.BlockSpec((tm, n), lambda i, k: (i, 0)),
        out_shape=jax.ShapeDtypeStruct((m, n), F32),
        compiler_params=_cparams(2, 2 * (n_lhs * tm * tk * 2 + tk * n * 2 + tm * n * 4) + tm * n * 4),
        name=name,
    )(*xs, w)


def _ln_kernel(h_ref, mix_ref, g_ref, lng_ref, lnb_ref, sc_ref, sh_ref, ho_ref, hm_ref, *, alpha):
    r = alpha * h_ref[0] + g_ref[0] * mix_ref[0]
    mu = jnp.mean(r, axis=-1, keepdims=True)
    rc = r - mu
    var = jnp.mean(rc * rc, axis=-1, keepdims=True)
    y = rc * lax.rsqrt(var + LN_EPS) * lng_ref[...] + lnb_ref[...]
    ho_ref[0] = y
    hm_ref[0] = (y * (1.0 + sc_ref[0]) + sh_ref[0]).astype(BF16)


def residual_layernorm(h, mix, gate, ln_g, ln_b, sc, sh, *, alpha, tm=256):
    b, l, d = h.shape
    tm = min(tm, l)
    row = pl.BlockSpec((1, tm, d), lambda bb, i: (bb, i, 0))
    vec_b = pl.BlockSpec((1, 1, d), lambda bb, i: (bb, 0, 0))
    vec = pl.BlockSpec((1, d), lambda bb, i: (0, 0))
    return pl.pallas_call(
        functools.partial(_ln_kernel, alpha=alpha),
        grid=(b, l // tm),
        in_specs=[row, row, vec_b, vec, vec, vec_b, vec_b],
        out_specs=[row, row],
        out_shape=[jax.ShapeDtypeStruct((b, l, d), F32), jax.ShapeDtypeStruct((b, l, d), BF16)],
        compiler_params=_cparams(2, 2 * tm * d * (4 + 4 + 4 + 2) + 4 * tm * d * 4),
        name="residual_layernorm",
    )(h, mix.reshape(b, l, d), gate, ln_g.reshape(1, d), ln_b.reshape(1, d), sc, sh)


def _dft_cos_sin(n):
    k = np.arange(n, dtype=np.int64)
    ang = (2.0 * np.pi / n) * ((k[:, None] * k[None, :]) % n).astype(np.float64)
    return jnp.asarray(np.cos(ang), F32), jnp.asarray(np.sin(ang), F32)


def _channel_dft(x, cd_ref, sd_ref, hd):
    ps, qs = [], []
    for hh in range(x.shape[1] // hd):
        xh = x[:, hh * hd:(hh + 1) * hd]
        ps.append(jnp.dot(xh, cd_ref[...], precision=HIGHEST, preferred_element_type=F32))
        qs.append(jnp.dot(xh, sd_ref[...], precision=HIGHEST, preferred_element_type=F32))
    return jnp.concatenate(ps, axis=1), jnp.concatenate(qs, axis=1)


def _fnet_a_kernel(x_ref, cd_ref, sd_ref, c1_ref, s1_ref, twr_ref, twi_ref, ar_ref, ai_ref, *, hd):
    p, q = _channel_dft(x_ref[0], cd_ref, sd_ref, hd)
    dot = functools.partial(jnp.dot, precision=HIGHEST, preferred_element_type=F32)
    c1, s1 = c1_ref[...], s1_ref[...]
    ar = dot(c1, p) - dot(s1, q)
    ai = -(dot(c1, q) + dot(s1, p))
    reps = ar.shape[1] // V7X_LANES
    tr = jnp.tile(twr_ref[0], (1, reps))
    ti = jnp.tile(twi_ref[0], (1, reps))
    ar_ref[0, 0] = ar * tr - ai * ti
    ai_ref[0, 0] = ar * ti + ai * tr


def _fnet_c_kernel(ar_ref, ai_ref, c2_ref, s2_ref, o_ref, *, scale):
    dot = functools.partial(jnp.dot, precision=HIGHEST, preferred_element_type=F32)
    y = dot(c2_ref[...], ar_ref[0]) + dot(s2_ref[...], ai_ref[0])
    o_ref[0] = (y * scale).astype(o_ref.dtype)


def _fnet_dense_kernel(x_ref, cd_ref, sd_ref, cl_ref, sl_ref, o_ref, *, hd, scale):
    p, q = _channel_dft(x_ref[0], cd_ref, sd_ref, hd)
    dot = functools.partial(jnp.dot, precision=HIGHEST, preferred_element_type=F32)
    y = dot(cl_ref[...], p) - dot(sl_ref[...], q)
    o_ref[0] = (y * scale).astype(o_ref.dtype)


def fnet_mix(u, colblk, width):
    b, l, ctot = u.shape
    hd = width // FNET_HEADS
    scale = 1.0 / math.sqrt(l * hd)
    cd, sd = _dft_cos_sin(hd)
    full = lambda shape: pl.BlockSpec(shape, lambda *_: (0,) * len(shape))
    n2 = 128
    if l < 8 * n2:
        cl, sl = _dft_cos_sin(l)
        return pl.pallas_call(
            functools.partial(_fnet_dense_kernel, hd=hd, scale=scale),
            grid=(b,),
            in_specs=[pl.BlockSpec((1, l, width), lambda bb: (bb, 0, colblk)),
                      full((hd, hd)), full((hd, hd)), full((l, l)), full((l, l))],
            out_specs=pl.BlockSpec((1, l, width), lambda bb: (bb, 0, 0)),
            out_shape=jax.ShapeDtypeStruct((b, l, width), BF16),
            compiler_params=_cparams(1, 16 * l * width * 4),
            name="fnet_dense",
        )(u, cd, sd, cl, sl)
    n1 = l // n2
    c1, s1 = _dft_cos_sin(n1)
    c2, s2 = _dft_cos_sin(n2)
    kk = (np.arange(n2, dtype=np.int64)[:, None] * np.arange(n1, dtype=np.int64)[None, :]) % l
    ang = (2.0 * np.pi / l) * kk.astype(np.float64)
    twr = jnp.asarray(np.broadcast_to(np.cos(ang)[:, :, None], (n2, n1, V7X_LANES)), F32)
    twi = jnp.asarray(np.broadcast_to(-np.sin(ang)[:, :, None], (n2, n1, V7X_LANES)), F32)
    cpb = ctot // width
    ar, ai = pl.pallas_call(
        functools.partial(_fnet_a_kernel, hd=hd),
        grid=(b, n2),
        in_specs=[pl.BlockSpec((1, n1, width), lambda bb, j: (bb, 0, j * cpb + colblk)),
                  full((hd, hd)), full((hd, hd)), full((n1, n1)), full((n1, n1)),
                  pl.BlockSpec((1, n1, V7X_LANES), lambda bb, j: (j, 0, 0)),
                  pl.BlockSpec((1, n1, V7X_LANES), lambda bb, j: (j, 0, 0))],
        out_specs=[pl.BlockSpec((1, 1, n1, width), lambda bb, j: (bb, j, 0, 0))] * 2,
        out_shape=[jax.ShapeDtypeStruct((b, n2, n1, width), F32)] * 2,
        compiler_params=_cparams(2, 24 * n1 * width * 4),
        name="fnet_stage_a",
    )(u.reshape(b, n1, n2 * ctot), cd, sd, c1, s1, twr, twi)
    y = pl.pallas_call(
        functools.partial(_fnet_c_kernel, scale=scale),
        grid=(b, n1),
        in_specs=[pl.BlockSpec((1, n2, width), lambda bb, j: (bb, 0, j)),
                  pl.BlockSpec((1, n2, width), lambda bb, j: (bb, 0, j)),
                  full((n2, n2)), full((n2, n2))],
        out_specs=pl.BlockSpec((1, n2, width), lambda bb, j: (bb, 0, j)),
        out_shape=jax.ShapeDtypeStruct((b, n2, n1 * width), BF16),
        compiler_params=_cparams(2, 16 * n2 * width * 4),
        name="fnet_stage_c",
    )(ar.reshape(b, n2, n1 * width), ai.reshape(b, n2, n1 * width), c2, s2)
    return y.reshape(b, l, width)


S5_LANE_BLOCK_GROUPS = V7X_LANES // S5_GROUP_CH
S5_STATE_BLOCK = S5_LANE_BLOCK_GROUPS * S5_STATE
S5_LEVELS = (1, 2, 4)


def _s5_direction_params(p, k, reverse):
    g, st = p["s5_lam_re"].shape[1:]
    lr, li = p["s5_lam_re"][k].astype(F32), p["s5_lam_im"][k].astype(F32)
    dt = jnp.exp(p["s5_log_dt"][k].astype(F32))[:, None]
    zr, zi = lr * dt, li * dt

    def power(s):
        e = jnp.exp(s * zr)
        return e * jnp.cos(s * zi), e * jnp.sin(s * zi)

    ar, ai = power(1.0)
    nr, ni = ar - 1.0, ai
    den = lr * lr + li * li
    cr, ci = (nr * lr + ni * li) / den, (ni * lr - nr * li) / den
    b_re, b_im = p["s5_b_re"].astype(F32), p["s5_b_im"].astype(F32)
    btr = b_re * cr[..., None] - b_im * ci[..., None]
    bti = b_re * ci[..., None] + b_im * cr[..., None]
    nj = g // S5_LANE_BLOCK_GROUPS
    eye = jnp.eye(S5_LANE_BLOCK_GROUPS, dtype=F32)

    def blockdiag_b(bt):
        b4 = bt.reshape(nj, S5_LANE_BLOCK_GROUPS, st, S5_GROUP_CH)
        return jnp.einsum("jgpc,gh->jgchp", b4, eye).reshape(nj, V7X_LANES, S5_STATE_BLOCK).astype(BF16)

    t = jnp.arange(V7X_SUBLANES)
    tabs_r, tabs_i = [], []
    for s in S5_LEVELS:
        pr, pi = power(float(s))
        keep = (t <= V7X_SUBLANES - 1 - s) if reverse else (t >= s)
        tabs_r.append(jnp.where(keep[:, None], pr.reshape(1, -1), 0.0))
        tabs_i.append(jnp.where(keep[:, None], pi.reshape(1, -1), 0.0))
    steps = (V7X_SUBLANES - t) if reverse else (t + 1)
    e = jnp.exp(steps[:, None, None].astype(F32) * zr[None])
    tabs_r.append((e * jnp.cos(steps[:, None, None].astype(F32) * zi[None])).reshape(V7X_SUBLANES, -1))
    tabs_i.append((e * jnp.sin(steps[:, None, None].astype(F32) * zi[None])).reshape(V7X_SUBLANES, -1))
    return blockdiag_b(btr), blockdiag_b(bti), jnp.stack(tabs_r), jnp.stack(tabs_i)


def _s5_output_params(p):
    g, ch, st = p["s5_c_re"].shape
    nj = g // S5_LANE_BLOCK_GROUPS
    eye = jnp.eye(S5_LANE_BLOCK_GROUPS, dtype=F32)

    def blockdiag_c(c):
        c4 = c.astype(F32).reshape(nj, S5_LANE_BLOCK_GROUPS, ch, st)
        return jnp.einsum("jgcp,gh->jgphc", c4, eye).reshape(nj, S5_STATE_BLOCK, V7X_LANES).astype(BF16)

    return blockdiag_c(p["s5_c_re"]), blockdiag_c(p["s5_c_im"])


def _s5_kernel(*refs, reverse, finish, tt, nj):
    if finish:
        (u_ref, h0r_ref, h0i_ref, bre_ref, bim_ref, cre_ref, cim_ref, tr_ref, ti_ref,
         yf_ref, d_ref, wglu_ref, y_ref, hfr_ref, hfi_ref, xr_s, xi_s, car_s, cai_s, yb_s) = refs
    else:
        (u_ref, h0r_ref, h0i_ref, bre_ref, bim_ref, cre_ref, cim_ref, tr_ref, ti_ref,
         y_ref, hfr_ref, hfi_ref, xr_s, xi_s, car_s, cai_s) = refs
        yb_s = y_ref.at[0]

    @pl.when(pl.program_id(1) == 0)
    def _():
        car_s[...] = h0r_ref[0]
        cai_s[...] = h0i_ref[0]

    nblk = tt // V7X_SUBLANES
    sb = S5_STATE_BLOCK
    edge = 0 if reverse else V7X_SUBLANES - 1
    for j in range(nj):
        lanes = slice(j * V7X_LANES, (j + 1) * V7X_LANES)
        states = slice(j * sb, (j + 1) * sb)
        ub = u_ref[0, :, lanes].astype(BF16)
        xr_s[...] = jnp.dot(ub, bre_ref[j], preferred_element_type=F32)
        xi_s[...] = jnp.dot(ub, bim_ref[j], preferred_element_type=F32)

        def rbody(r, carry, states=states):
            cr, ci = carry
            rr = (nblk - 1 - r) if reverse else r
            row0 = pl.multiple_of(rr * V7X_SUBLANES, V7X_SUBLANES)
            xr = xr_s[pl.ds(row0, V7X_SUBLANES), :]
            xi = xi_s[pl.ds(row0, V7X_SUBLANES), :]
            for lvl, s in enumerate(S5_LEVELS):
                ar = tr_ref[lvl, :, states]
                ai = ti_ref[lvl, :, states]
                shift = (V7X_SUBLANES - s) if reverse else s
                sr = pltpu.roll(xr, shift, 0)
                si = pltpu.roll(xi, shift, 0)
                xr, xi = xr + ar * sr - ai * si, xi + ar * si + ai * sr
            pr = tr_ref[len(S5_LEVELS), :, states]
            pi = ti_ref[len(S5_LEVELS), :, states]
            crb = jnp.broadcast_to(cr, xr.shape)
            cib = jnp.broadcast_to(ci, xi.shape)
            xr, xi = xr + pr * crb - pi * cib, xi + pr * cib + pi * crb
            xr_s[pl.ds(row0, V7X_SUBLANES), :] = xr
            xi_s[pl.ds(row0, V7X_SUBLANES), :] = xi
            return xr[edge:edge + 1, :], xi[edge:edge + 1, :]

        cr, ci = lax.fori_loop(0, nblk, rbody, (car_s[:, states], cai_s[:, states]))
        car_s[:, states] = cr
        cai_s[:, states] = ci
        yb_s[:, lanes] = (jnp.dot(xr_s[...].astype(BF16), cre_ref[j], preferred_element_type=F32)
                          - jnp.dot(xi_s[...].astype(BF16), cim_ref[j], preferred_element_type=F32))

    hfr_ref[0] = car_s[...]
    hfi_ref[0] = cai_s[...]
    if finish:
        y = yf_ref[0] + yb_s[...] + d_ref[...] * u_ref[0]
        gl = _gelu_tanh(y)
        gate = jnp.dot(gl.astype(BF16), wglu_ref[...], preferred_element_type=F32)
        y_ref[0] = (gl * jax.nn.sigmoid(gate)).astype(y_ref.dtype)


def s5_scan(u, colblk, width, dirp, outp, h0, *, reverse, fin=None, tt=256):
    b, l, _ = u.shape
    tt = min(tt, l)
    nc = l // tt
    nj = width // V7X_LANES
    ns = nj * S5_STATE_BLOCK
    bre, bim, tr, ti = dirp
    cre, cim = outp
    h0r, h0i = h0
    cmap = (lambda bb, c: (bb, nc - 1 - c, colblk)) if reverse else (lambda bb, c: (bb, c, colblk))
    omap = (lambda bb, c: (bb, nc - 1 - c, 0)) if reverse else (lambda bb, c: (bb, c, 0))
    full = lambda shape: pl.BlockSpec(shape, lambda *_: (0,) * len(shape))
    st_spec = pl.BlockSpec((1, 1, ns), lambda bb, c: (bb, 0, 0))
    in_specs = [pl.BlockSpec((1, tt, width), cmap), st_spec, st_spec,
                full(bre.shape), full(bim.shape), full(cre.shape), full(cim.shape),
                full(tr.shape), full(ti.shape)]
    args = [u, h0r, h0i, bre, bim, cre, cim, tr, ti]
    scratch = [pltpu.VMEM((tt, S5_STATE_BLOCK), F32), pltpu.VMEM((tt, S5_STATE_BLOCK), F32),
               pltpu.VMEM((1, ns), F32), pltpu.VMEM((1, ns), F32)]
    if fin is not None:
        y_other, d, wglu = fin
        in_specs += [pl.BlockSpec((1, tt, width), omap), full((1, width)), full(wglu.shape)]
        args += [y_other, d.reshape(1, width).astype(F32), wglu]
        scratch.append(pltpu.VMEM((tt, width), F32))
    out_dtype = BF16 if fin is not None else F32
    return pl.pallas_call(
        functools.partial(_s5_kernel, reverse=reverse, finish=fin is not None, tt=tt, nj=nj),
        grid=(b, nc),
        in_specs=in_specs,
        out_specs=[pl.BlockSpec((1, tt, width), omap), st_spec, st_spec],
        out_shape=[jax.ShapeDtypeStruct((b, l, width), out_dtype),
                   jax.ShapeDtypeStruct((b, 1, ns), F32), jax.ShapeDtypeStruct((b, 1, ns), F32)],
        scratch_shapes=scratch,
        compiler_params=_cparams(2, 8 * tt * width * 4 + 16 * nj * V7X_LANES * S5_STATE_BLOCK * 2
                                 + 4 * 8 * 8 * ns * 4 + 4 * width * width),
        name="s5_scan_bwd" if reverse else "s5_scan_fwd",
    )(*args)


def s5_mix(u, colblk, width, p, h0f, h0b):
    outp = _s5_output_params(p)
    yf, hf_r, hf_i = s5_scan(u, colblk, width, _s5_direction_params(p, 0, False), outp, h0f, reverse=False)
    y, hb_r, hb_i = s5_scan(u, colblk, width, _s5_direction_params(p, 1, True), outp, h0b, reverse=True,
                            fin=(yf, p["s5_d"], p["s5_w_glu"].astype(BF16)))
    return y, (hf_r, hf_i), (hb_r, hb_i)


def _fill_padded(pad_s, main, prev, nxt, halo):
    i = pl.program_id(1)
    n = pl.num_programs(1)
    tt = main.shape[0]
    pad_s[0:halo, :] = jnp.where(i > 0, prev, 0.0)
    pad_s[halo:halo + tt, :] = main
    pad_s[halo + tt:halo + tt + halo, :] = jnp.where(i < n - 1, nxt, 0.0)


def _dwconv_rows(pad_s, w_ref, width, halo, tt, row_block=32):
    off = halo - (width - 1) // 2
    outs = []
    for r0 in range(0, tt, row_block):
        rb = min(row_block, tt - r0)
        acc = None
        for k in range(width):
            term = w_ref[k:k + 1, :] * pad_s[r0 + off + k:r0 + off + k + rb, :]
            acc = term if acc is None else acc + term
        outs.append(acc)
    return outs


def _halo_specs(tt, halo, width, colblk, l):
    r = tt // halo
    nh = l // halo
    main = pl.BlockSpec((1, tt, width), lambda bb, i: (bb, i, colblk))
    prev = pl.BlockSpec((1, halo, width), lambda bb, i: (bb, jnp.maximum(i * r - 1, 0), colblk))
    nxt = pl.BlockSpec((1, halo, width), lambda bb, i: (bb, jnp.minimum((i + 1) * r, nh - 1), colblk))
    return [main, prev, nxt]


CV_HALO = 16


def _conv_module_kernel(v_ref, vp_ref, vn_ref, g_ref, gp_ref, gn_ref, w_ref, b_ref, lg_ref, lb_ref,
                        o_ref, pad_s, *, tt):
    glu = lambda v, g: v * jax.nn.sigmoid(g)
    _fill_padded(pad_s, glu(v_ref[0], g_ref[0]), glu(vp_ref[0], gp_ref[0]), glu(vn_ref[0], gn_ref[0]), CV_HALO)
    row_block = 32
    for bi, acc in enumerate(_dwconv_rows(pad_s, w_ref, CONV_WIDTH, CV_HALO, tt, row_block)):
        hcv = acc + b_ref[...]
        mu = jnp.mean(hcv, axis=-1, keepdims=True)
        hc = hcv - mu
        var = jnp.mean(hc * hc, axis=-1, keepdims=True)
        yn = hc * lax.rsqrt(var + LN_EPS) * lg_ref[...] + lb_ref[...]
        o_ref[0, bi * row_block:bi * row_block + acc.shape[0], :] = _silu(yn).astype(o_ref.dtype)


def conv_module(u, colblk_v, colblk_g, width, p, tt=128):
    b, l, _ = u.shape
    tt = min(tt, l)
    wk = jnp.zeros((32, width), F32).at[:CONV_WIDTH].set(p["cv_w"].astype(F32))
    vec = pl.BlockSpec((1, width), lambda bb, i: (0, 0))
    return pl.pallas_call(
        functools.partial(_conv_module_kernel, tt=tt),
        grid=(b, l // tt),
        in_specs=_halo_specs(tt, CV_HALO, width, colblk_v, l) + _halo_specs(tt, CV_HALO, width, colblk_g, l)
        + [pl.BlockSpec((32, width), lambda bb, i: (0, 0)), vec, vec, vec],
        out_specs=pl.BlockSpec((1, tt, width), lambda bb, i: (bb, i, 0)),
        out_shape=jax.ShapeDtypeStruct((b, l, width), BF16),
        scratch_shapes=[pltpu.VMEM((tt + 2 * CV_HALO, width), F32)],
        compiler_params=_cparams(2, 12 * (tt + 2 * CV_HALO) * width * 4),
        name="conv_module",
    )(u, u, u, u, u, u, wk, p["cv_b"].reshape(1, width).astype(F32),
      p["cv_ln_g"].reshape(1, width).astype(F32), p["cv_ln_b"].reshape(1, width).astype(F32))


SSD_HALO = 8
SSD_HEADS_PER_GROUP = 4
SSD_PAIR = V7X_LANES // SSD_HEAD_DIM


def _ssd_conv_kernel(x_ref, xp_ref, xn_ref, w_ref, b_ref, o_ref, pad_s, *, tt):
    _fill_padded(pad_s, x_ref[0], xp_ref[0], xn_ref[0], SSD_HALO)
    row_block = 64
    for bi, acc in enumerate(_dwconv_rows(pad_s, w_ref, SSD_CONV, SSD_HALO, tt, row_block)):
        o_ref[0, bi * row_block:bi * row_block + acc.shape[0], :] = _silu(acc + b_ref[...])


def ssd_conv(u, colblk0, ncol, width, conv_w, conv_b, tt=256):
    b, l, _ = u.shape
    tt = min(tt, l)
    r = tt // SSD_HALO
    nh = l // SSD_HALO
    wk = jnp.zeros((V7X_SUBLANES, ncol * width), F32).at[:SSD_CONV].set(conv_w.astype(F32))
    return pl.pallas_call(
        functools.partial(_ssd_conv_kernel, tt=tt),
        grid=(b, l // tt, ncol),
        in_specs=[pl.BlockSpec((1, tt, width), lambda bb, i, c: (bb, i, colblk0 + c)),
                  pl.BlockSpec((1, SSD_HALO, width), lambda bb, i, c: (bb, jnp.maximum(i * r - 1, 0), colblk0 + c)),
                  pl.BlockSpec((1, SSD_HALO, width),
                               lambda bb, i, c: (bb, jnp.minimum((i + 1) * r, nh - 1), colblk0 + c)),
                  pl.BlockSpec((V7X_SUBLANES, width), lambda bb, i, c: (0, c)),
                  pl.BlockSpec((1, width), lambda bb, i, c: (0, c))],
        out_specs=pl.BlockSpec((1, tt, width), lambda bb, i, c: (bb, i, c)),
        out_shape=jax.ShapeDtypeStruct((b, l, ncol * width), F32),
        scratch_shapes=[pltpu.VMEM((tt + 2 * SSD_HALO, width), F32)],
        compiler_params=_cparams(3, 8 * (tt + 2 * SSD_HALO) * width * 4),
        name="ssd_conv",
    )(u, u, u, wk, conv_b.reshape(1, -1).astype(F32))


def _ssd_kernel(*refs, reverse, finish, q, gw, nheads):
    if finish:
        (xbc_ref, dtr_ref, dtb_ref, arow_ref, s0_ref, yf_ref, z_ref, dvec_ref, nw_ref,
         y_ref, sfin_ref, st_s, yb_s) = refs
    else:
        xbc_ref, dtr_ref, dtb_ref, arow_ref, s0_ref, y_ref, sfin_ref, st_s = refs
        yb_s = y_ref.at[0]

    @pl.when(pl.program_id(1) == 0)
    def _():
        st_s[...] = s0_ref[0]

    n = SSD_STATE
    dt = _softplus(dtr_ref[0] + dtb_ref[...])
    dta = dt * arow_ref[...]
    row = lax.broadcasted_iota(jnp.int32, (q, q), 0)
    col = lax.broadcasted_iota(jnp.int32, (q, q), 1)
    tri = (row <= col) if reverse else (row >= col)
    cum = jnp.dot(tri.astype(F32), dta, precision=HIGHEST, preferred_element_type=F32)
    cum_t = cum.T
    dt_t = dt.T
    last = cum[0:1, :] if reverse else cum[q - 1:q, :]
    wdt = jnp.exp(last - cum) * dt
    ecum = jnp.exp(cum)
    elast = jnp.exp(last)
    lo_q = lax.broadcasted_iota(jnp.int32, (q, V7X_LANES), 1) < SSD_HEAD_DIM
    lo_n = lax.broadcasted_iota(jnp.int32, (n, V7X_LANES), 1) < SSD_HEAD_DIM
    b_off = nheads * SSD_HEAD_DIM
    c_off = b_off + SSD_NGROUPS * n
    for g in range(SSD_NGROUPS):
        bg = xbc_ref[0, :, b_off + g * n:b_off + (g + 1) * n]
        cb = xbc_ref[0, :, c_off + g * n:c_off + (g + 1) * n].astype(BF16)
        scores = lax.dot_general(cb, bg.astype(BF16), (((1,), (1,)), ((), ())), preferred_element_type=F32)
        for pr in range(SSD_HEADS_PER_GROUP // SSD_PAIR):
            pidx = g * (SSD_HEADS_PER_GROUP // SSD_PAIR) + pr
            lanes = slice(pidx * V7X_LANES, (pidx + 1) * V7X_LANES)
            xp = xbc_ref[0, :, lanes].astype(BF16)
            yd, stn = [], []
            for hh in range(SSD_PAIR):
                h = pidx * SSD_PAIR + hh
                seg = cum[:, h:h + 1] - cum_t[h:h + 1, :]
                dec = jnp.where(tri, jnp.exp(jnp.where(tri, seg, 0.0)), 0.0)
                w = (scores * dec * dt_t[h:h + 1, :]).astype(BF16)
                yd.append(jnp.dot(w, xp, preferred_element_type=F32))
                wb = (bg * wdt[:, h:h + 1]).T.astype(BF16)
                stn.append(jnp.dot(wb, xp, preferred_element_type=F32))
            h0 = pidx * SSD_PAIR
            s_prev = st_s[pidx]
            y_off = jnp.dot(cb, s_prev.astype(BF16), preferred_element_type=F32)
            y_off = y_off * jnp.where(lo_q, ecum[:, h0:h0 + 1], ecum[:, h0 + 1:h0 + 2])
            st_s[pidx] = (s_prev * jnp.where(lo_n, elast[:, h0:h0 + 1], elast[:, h0 + 1:h0 + 2])
                          + jnp.where(lo_n, stn[0], stn[1]))
            yb_s[:, lanes] = jnp.where(lo_q, yd[0], yd[1]) + y_off
    sfin_ref[0] = st_s[...]
    if finish:
        xs = xbc_ref[0, :, :b_off]
        y = yf_ref[0] + yb_s[...] + dvec_ref[...] * xs
        t = y * _silu(z_ref[0])
        for g in range(SSD_NGROUPS):
            tg = t[:, g * gw:(g + 1) * gw]
            ms = jnp.mean(tg * tg, axis=-1, keepdims=True)
            y_ref[0, :, g * gw:(g + 1) * gw] = (tg * lax.rsqrt(ms + RMS_EPS)
                                                * nw_ref[:, g * gw:(g + 1) * gw]).astype(y_ref.dtype)


def ssd_scan(xbc, dtraw, dcol, dt_bias, a_head, s0, *, reverse, fin=None, q=256):
    b, l, cw = xbc.shape
    q = min(q, l)
    nc = l // q
    nheads = dt_bias.shape[0]
    width = nheads * SSD_HEAD_DIM
    npair = nheads // SSD_PAIR
    pad = lambda v: jnp.zeros((1, V7X_LANES), F32).at[0, :nheads].set(v.astype(F32))
    cmap = (lambda bb, c: (bb, nc - 1 - c, 0)) if reverse else (lambda bb, c: (bb, c, 0))
    dmap = (lambda bb, c: (bb, nc - 1 - c, dcol)) if reverse else (lambda bb, c: (bb, c, dcol))
    full = lambda shape: pl.BlockSpec(shape, lambda *_: (0,) * len(shape))
    st_spec = pl.BlockSpec((1, npair, SSD_STATE, V7X_LANES), lambda bb, c: (bb, 0, 0, 0))
    in_specs = [pl.BlockSpec((1, q, cw), cmap), pl.BlockSpec((1, q, V7X_LANES), dmap),
                full((1, V7X_LANES)), full((1, V7X_LANES)), st_spec]
    args = [xbc, dtraw, pad(dt_bias), pad(a_head), s0]
    scratch = [pltpu.VMEM((npair, SSD_STATE, V7X_LANES), F32)]
    if fin is not None:
        y_other, z, zcol, dvec, nw = fin
        zmap = (lambda bb, c: (bb, nc - 1 - c, zcol)) if reverse else (lambda bb, c: (bb, c, zcol))
        in_specs += [pl.BlockSpec((1, q, width), cmap), pl.BlockSpec((1, q, width), zmap),
                     full((1, width)), full((1, width))]
        args += [y_other, z, dvec, nw]
        scratch.append(pltpu.VMEM((q, width), F32))
    out_dtype = BF16 if fin is not None else F32
    return pl.pallas_call(
        functools.partial(_ssd_kernel, reverse=reverse, finish=fin is not None, q=q,
                          gw=width // SSD_NGROUPS, nheads=nheads),
        grid=(b, nc),
        in_specs=in_specs,
        out_specs=[pl.BlockSpec((1, q, width), cmap), st_spec],
        out_shape=[jax.ShapeDtypeStruct((b, l, width), out_dtype),
                   jax.ShapeDtypeStruct((b, npair, SSD_STATE, V7X_LANES), F32)],
        scratch_shapes=scratch,
        compiler_params=_cparams(2, 4 * q * cw * 4 + 12 * q * width * 4 + 24 * q * q * 4),
        name="ssd_scan_bwd" if reverse else "ssd_scan_fwd",
    )(*args)


def ssd_mix(u, dtraw, zcol, xcol0, width, p, s0f, s0b):
    nheads = p["ssd_dt_bias"].shape[1]
    xbc = ssd_conv(u, xcol0, 2, width, p["ssd_conv_w"], p["ssd_conv_b"])
    a_head = -jnp.exp(p["ssd_a_log"].astype(F32))
    yf, sf = ssd_scan(xbc, dtraw, 0, p["ssd_dt_bias"][0], a_head[0], s0f, reverse=False)
    dvec = jnp.repeat(p["ssd_d"].astype(F32), SSD_HEAD_DIM).reshape(1, width)
    y, sb = ssd_scan(xbc, dtraw, 1, p["ssd_dt_bias"][1], a_head[1], s0b, reverse=True,
                     fin=(yf, u, zcol, dvec, p["ssd_norm_w"].reshape(1, width).astype(F32)))
    return y, sf, sb


def _pad_cols(w, n):
    return jnp.pad(w, ((0, 0), (0, n - w.shape[1])))


def _layer_weights(p, d, hidden_pad):
    gw = d // N_MIXERS
    nheads = p["ssd_dt_bias"].shape[1]
    n_main = 4 * gw + gw + (gw + 2 * SSD_NGROUPS * SSD_STATE)
    w_in = p["w_in"]
    w_main = w_in[:, :n_main].astype(BF16)
    w_dt = jnp.concatenate([_pad_cols(w_in[:, n_main:n_main + nheads], V7X_LANES),
                            _pad_cols(w_in[:, n_main + nheads:], V7X_LANES)], axis=1).astype(BF16)
    hidden = p["w_gate"].shape[1]
    return dict(
        w_main=w_main, w_dt=w_dt, w_out=p["w_out"].astype(BF16),
        w_gate=_pad_cols(p["w_gate"], hidden_pad).astype(BF16),
        w_up=_pad_cols(p["w_up"], hidden_pad).astype(BF16),
        w_down=jnp.pad(p["w_down"], ((0, hidden_pad - hidden), (0, 0))).astype(BF16))


def _mixers(hm, wts, p, s5_h0, ssd_s0, want_output=True):
    b, l, d = hm.shape
    gw = d // N_MIXERS
    hm2 = hm.reshape(b * l, d)
    tm = 1024 if b * l >= 1024 else b * l
    u = matmul(hm2, wts["w_main"], tm=tm, tn=1024, name="w_in_main").reshape(b, l, -1)
    dtraw = matmul(hm2, wts["w_dt"], tm=tm, tn=2 * V7X_LANES, name="w_in_dt").reshape(b, l, -1)
    ys, s5f, s5b = s5_mix(u, 1, gw, p, *s5_h0)
    yd, sdf, sdb = ssd_mix(u, dtraw, 4, 5, gw, p, *ssd_s0)
    states = ((s5f, s5b), (sdf, sdb))
    if not want_output:
        return None, states
    ya = fnet_mix(u, 0, gw)
    yc = conv_module(u, 2, 3, gw, p)
    return [y.reshape(b * l, gw) for y in (ya, ys, yc, yd)], states


def _ffn(hm, wts, tm):
    hid = gate_up(hm, wts["w_gate"], wts["w_up"], tm=tm, tn=512)
    return matmul_ktiled(hid, wts["w_down"], tm=min(tm, 512), tk=1024, name="w_down")


def kernel(x, c, ctx, c_ctx, w_ada, b_ada, w_in, s5_lam_re, s5_lam_im, s5_log_dt, s5_b_re, s5_b_im,
           s5_c_re, s5_c_im, s5_d, s5_w_glu, cv_w, cv_b, cv_ln_g, cv_ln_b, ssd_conv_w, ssd_conv_b,
           ssd_a_log, ssd_dt_bias, ssd_d, ssd_norm_w, w_out, ln1_g, ln1_b, w_gate, w_up, w_down,
           ln2_g, ln2_b):
    bsz, seq, d = x.shape
    cl = ctx.shape[1]
    depth = w_in.shape[0]
    gw = d // N_MIXERS
    alpha = (2 * depth) ** 0.25
    hidden = w_gate.shape[2]
    hidden_pad = -(-hidden // 1024) * 1024
    per_layer = dict(w_in=w_in, w_out=w_out, s5_lam_re=s5_lam_re, s5_lam_im=s5_lam_im, s5_log_dt=s5_log_dt,
                     s5_b_re=s5_b_re, s5_b_im=s5_b_im, s5_c_re=s5_c_re, s5_c_im=s5_c_im, s5_d=s5_d,
                     s5_w_glu=s5_w_glu, cv_w=cv_w, cv_b=cv_b, cv_ln_g=cv_ln_g, cv_ln_b=cv_ln_b,
                     ssd_conv_w=ssd_conv_w, ssd_conv_b=ssd_conv_b, ssd_a_log=ssd_a_log,
                     ssd_dt_bias=ssd_dt_bias, ssd_d=ssd_d, ssd_norm_w=ssd_norm_w,
                     w_gate=w_gate, w_up=w_up, w_down=w_down)

    c8 = jnp.zeros((V7X_SUBLANES, d), F32).at[:bsz].set(c).at[bsz].set(c_ctx)
    mods = ada_modulation(c8, w_ada, b_ada)

    def mod_vectors(i):
        m = mods[i].reshape(V7X_SUBLANES, 6, d)
        lat = [m[:bsz, k].reshape(bsz, 1, d) for k in range(6)]
        cx = [jnp.broadcast_to(m[bsz, k].reshape(1, 1, d), (bsz, 1, d)) for k in range(6)]
        return lat, cx

    nstate = (gw // V7X_LANES) * S5_STATE_BLOCK
    npair = ssd_dt_bias.shape[2] // SSD_PAIR
    z_s5 = (jnp.zeros((bsz, 1, nstate), F32), jnp.zeros((bsz, 1, nstate), F32))
    z_ssd = jnp.zeros((bsz, npair, SSD_STATE, V7X_LANES), F32)

    lat0, cx0 = mod_vectors(0)
    h, hm = prologue(x, lat0[1], lat0[0])
    hc = ctx
    hcm = modulate(ctx, cx0[1], cx0[0])
    for i in range(depth):
        with_ctx = i < depth - 1
        p = {k: v[i] for k, v in per_layer.items()}
        wts = _layer_weights(p, d, hidden_pad)
        lat, cx = mod_vectors(i)
        lat_next, cx_next = mod_vectors(min(i + 1, depth - 1))
        mix_c, ((c5f, c5b), (cdf, cdb)) = _mixers(hcm, wts, p, (z_s5, z_s5), (z_ssd, z_ssd),
                                                   want_output=with_ctx)
        mix_l, _ = _mixers(hm, wts, p, (c5f, c5b), (cdf, cdb))
        out = matmul_ktiled(mix_l, wts["w_out"], tm=512, name="w_out")
        h, hm2 = residual_layernorm(h, out, lat[2], ln1_g[i], ln1_b[i], lat[4], lat[3], alpha=alpha)
        ff = _ffn(hm2.reshape(bsz * seq, d), wts, 1024)
        h, hm = residual_layernorm(h, ff, lat[5], ln2_g[i], ln2_b[i], lat_next[1], lat_next[0], alpha=alpha)
        if with_ctx:
            out_c = matmul_ktiled(mix_c, wts["w_out"], tm=512, name="w_out_ctx")
            hc, hcm2 = residual_layernorm(hc, out_c, cx[2], ln1_g[i], ln1_b[i], cx[4], cx[3], alpha=alpha)
            ff_c = _ffn(hcm2.reshape(bsz * cl, d), wts, 512)
            hc, hcm = residual_layernorm(hc, ff_c, cx[5], ln2_g[i], ln2_b[i], cx_next[1], cx_next[0],
                                         alpha=alpha)
    return h.astype(x.dtype)
```

```python
import functools
import math

import jax
import jax.numpy as jnp
import numpy as np
from jax import lax
from jax.experimental import pallas as pl
from jax.experimental.pallas import tpu as pltpu

F32 = jnp.float32
BF16 = jnp.bfloat16
HIGHEST = lax.Precision.HIGHEST

GRID_W = 64
N_MIXERS = 4
FNET_HEADS = 4
S5_GROUP_CH = 16
S5_STATE = 64
CONV_WIDTH = 31
SSD_HEAD_DIM = 64
SSD_NGROUPS = 4
SSD_STATE = 128
SSD_CONV = 5
LN_EPS = 1e-5
RMS_EPS = 1e-5

V7X_LANES = 128
V7X_SUBLANES = 8
V7X_VMEM_LIMIT_BYTES = 60000 * 1024


def _cparams(n_axes, vmem_bytes):
    limit = int(min(max(vmem_bytes * 5 // 4 + (2 << 20), 16 << 20), V7X_VMEM_LIMIT_BYTES))
    return pltpu.CompilerParams(dimension_semantics=("arbitrary",) * n_axes,
                                vmem_limit_bytes=limit)


def _silu(x):
    return x * jax.nn.sigmoid(x)


def _gelu_tanh(x):
    return 0.5 * x * (1.0 + jnp.tanh(math.sqrt(2.0 / math.pi) * (x + 0.044715 * (x * x * x))))


def _softplus(x):
    return jnp.maximum(x, 0.0) + jnp.log1p(jnp.exp(-jnp.abs(x)))


def _ada_kernel(c_ref, w_ref, b_ref, o_ref):
    cs = _silu(c_ref[...])
    o_ref[0] = jnp.dot(cs, w_ref[0], precision=HIGHEST, preferred_element_type=F32) + b_ref[0]


def ada_modulation(c8, w_ada, b_ada, tn=1024):
    depth, d, n = w_ada.shape
    tn = min(tn, n)
    return pl.pallas_call(
        _ada_kernel,
        grid=(depth, n // tn),
        in_specs=[pl.BlockSpec((V7X_SUBLANES, d), lambda l, j: (0, 0)),
                  pl.BlockSpec((1, d, tn), lambda l, j: (l, 0, j)),
                  pl.BlockSpec((1, 1, tn), lambda l, j: (l, 0, j))],
        out_specs=pl.BlockSpec((1, V7X_SUBLANES, tn), lambda l, j: (l, 0, j)),
        out_shape=jax.ShapeDtypeStruct((depth, V7X_SUBLANES, n), F32),
        compiler_params=_cparams(2, 2 * d * tn * 4 + 6 * d * tn * 2),
        name="ada_modulation",
    )(c8, w_ada, b_ada.reshape(depth, 1, n))


def _prologue_kernel(x_ref, tr_ref, tc_ref, sc_ref, sh_ref, h_ref, hm_ref, *, tp, half):
    sc = 1.0 + sc_ref[0]
    sh = sh_ref[0]
    for rr in range(tp // GRID_W):
        rows = slice(rr * GRID_W, (rr + 1) * GRID_W)
        lo = x_ref[0, rows, :half] + tr_ref[rr:rr + 1, :]
        hi = x_ref[0, rows, half:] + tc_ref[...]
        h_ref[0, rows, :half] = lo
        h_ref[0, rows, half:] = hi
        hm_ref[0, rows, :half] = (lo * sc[:, :half] + sh[:, :half]).astype(BF16)
        hm_ref[0, rows, half:] = (hi * sc[:, half:] + sh[:, half:]).astype(BF16)


def prologue(x, sc, sh, tp=512):
    b, l, d = x.shape
    tp = min(tp, l)
    half = d // 2
    q = d // 4
    omega = jnp.exp(-math.log(10000.0) * jnp.arange(q, dtype=F32) / q)[None]
    r = jnp.arange(l // GRID_W, dtype=F32)[:, None]
    col = jnp.arange(GRID_W, dtype=F32)[:, None]
    tab_r = jnp.concatenate([jnp.sin(r * omega), jnp.cos(r * omega)], -1)
    tab_c = jnp.concatenate([jnp.sin(col * omega), jnp.cos(col * omega)], -1)
    rpt = tp // GRID_W
    return pl.pallas_call(
        functools.partial(_prologue_kernel, tp=tp, half=half),
        grid=(b, l // tp),
        in_specs=[pl.BlockSpec((1, tp, d), lambda bb, i: (bb, i, 0)),
                  pl.BlockSpec((rpt, half), lambda bb, i: (i, 0)),
                  pl.BlockSpec((GRID_W, half), lambda bb, i: (0, 0)),
                  pl.BlockSpec((1, 1, d), lambda bb, i: (bb, 0, 0)),
                  pl.BlockSpec((1, 1, d), lambda bb, i: (bb, 0, 0))],
        out_specs=[pl.BlockSpec((1, tp, d), lambda bb, i: (bb, i, 0)),
                   pl.BlockSpec((1, tp, d), lambda bb, i: (bb, i, 0))],
        out_shape=[jax.ShapeDtypeStruct((b, l, d), F32), jax.ShapeDtypeStruct((b, l, d), BF16)],
        compiler_params=_cparams(2, 2 * tp * d * (4 + 4 + 2)),
        name="prologue",
    )(x, tab_r, tab_c, sc, sh)


def _modulate_kernel(x_ref, sc_ref, sh_ref, o_ref):
    o_ref[0] = (x_ref[0] * (1.0 + sc_ref[0]) + sh_ref[0]).astype(BF16)


def modulate(x, sc, sh, tp=256):
    b, l, d = x.shape
    tp = min(tp, l)
    return pl.pallas_call(
        _modulate_kernel,
        grid=(b, l // tp),
        in_specs=[pl.BlockSpec((1, tp, d), lambda bb, i: (bb, i, 0)),
                  pl.BlockSpec((1, 1, d), lambda bb, i: (bb, 0, 0)),
                  pl.BlockSpec((1, 1, d), lambda bb, i: (bb, 0, 0))],
        out_specs=pl.BlockSpec((1, tp, d), lambda bb, i: (bb, i, 0)),
        out_shape=jax.ShapeDtypeStruct((b, l, d), BF16),
        compiler_params=_cparams(2, 2 * tp * d * 6),
        name="modulate",
    )(x, sc, sh)


def _mm_kernel(x_ref, w_ref, o_ref):
    o_ref[...] = jnp.dot(x_ref[...], w_ref[...], preferred_element_type=F32).astype(o_ref.dtype)


def matmul(x, w, *, tm=1024, tn=1024, out_dtype=F32, name="matmul"):
    m, k = x.shape
    n = w.shape[1]
    tm, tn = min(tm, m), min(tn, n)
    ob = jnp.dtype(out_dtype).itemsize
    return pl.pallas_call(
        _mm_kernel,
        grid=(n // tn, m // tm),
        in_specs=[pl.BlockSpec((tm, k), lambda j, i: (i, 0)),
                  pl.BlockSpec((k, tn), lambda j, i: (0, j))],
        out_specs=pl.BlockSpec((tm, tn), lambda j, i: (i, j)),
        out_shape=jax.ShapeDtypeStruct((m, n), out_dtype),
        compiler_params=_cparams(2, 2 * (tm * k * 2 + k * tn * 2 + tm * tn * ob) + tm * tn * 4),
        name=name,
    )(x, w)


def _gate_up_kernel(x_ref, wg_ref, wu_ref, o_ref):
    x = x_ref[...]
    g = jnp.dot(x, wg_ref[...], preferred_element_type=F32)
    u = jnp.dot(x, wu_ref[...], preferred_element_type=F32)
    o_ref[...] = (_silu(g) * u).astype(o_ref.dtype)


def gate_up(x, wg, wu, *, tm=1024, tn=512):
    m, k = x.shape
    n = wg.shape[1]
    tm, tn = min(tm, m), min(tn, n)
    return pl.pallas_call(
        _gate_up_kernel,
        grid=(n // tn, m // tm),
        in_specs=[pl.BlockSpec((tm, k), lambda j, i: (i, 0)),
                  pl.BlockSpec((k, tn), lambda j, i: (0, j)),
                  pl.BlockSpec((k, tn), lambda j, i: (0, j))],
        out_specs=pl.BlockSpec((tm, tn), lambda j, i: (i, j)),
        out_shape=jax.ShapeDtypeStruct((m, n), BF16),
        compiler_params=_cparams(2, 2 * (tm * k * 2 + 2 * k * tn * 2 + tm * tn * 2) + 3 * tm * tn * 4),
        name="gate_up",
    )(x, wg, wu)


def _mm_ktiled_kernel(x_ref, w_ref, o_ref):
    @pl.when(pl.program_id(1) == 0)
    def _():
        o_ref[...] = jnp.zeros_like(o_ref)

    o_ref[...] += jnp.dot(x_ref[...], w_ref[...], preferred_element_type=F32)


def matmul_ktiled(x, w, *, tm=512, tk=1024, name="matmul_ktiled"):
    m, kk = x.shape
    n = w.shape[1]
    tm, tk = min(tm, m), min(tk, kk)
    return pl.pallas_call(
        _mm_ktiled_kernel,
        grid=(m // tm, kk // tk),
        in_specs=[pl.BlockSpec((tm, tk), lambda i, k: (i, k)), pl.BlockSpec((tk, n), lambda i, k: (k, 0))],
        out_specs=pl.BlockSpec((tm, n), lambda i, k: (i, 0)),
        out_shape=jax.ShapeDtypeStruct((m, n), F32),
        compiler_params=_cparams(2, 2 * (tm * tk * 2 + tk * n * 2 + tm * n * 4) + tm * n * 4),
        name=name,
    )(x, w)


def _ln_kernel(h_ref, mix_ref, g_ref, lng_ref, lnb_ref, sc_ref, sh_ref, ho_ref, hm_ref, *, alpha):
    r = alpha * h_ref[...] + g_ref[0] * mix_ref[...]
    mu = jnp.mean(r, axis=-1, keepdims=True)
    rc = r - mu
    var = jnp.mean(rc * rc, axis=-1, keepdims=True)
    y = rc * lax.rsqrt(var + LN_EPS) * lng_ref[...] + lnb_ref[...]
    ho_ref[...] = y
    hm_ref[...] = (y * (1.0 + sc_ref[0]) + sh_ref[0]).astype(BF16)


def residual_layernorm(h, mix, gate, ln_g, ln_b, sc, sh, *, alpha, rows_per_batch, tm=256):
    m, n = h.shape
    tm = min(tm, m, rows_per_batch)
    row = pl.BlockSpec((tm, n), lambda i: (i, 0))
    vec_b = pl.BlockSpec((1, 1, n), lambda i: ((i * tm) // rows_per_batch, 0, 0))
    vec = pl.BlockSpec((1, n), lambda i: (0, 0))
    return pl.pallas_call(
        functools.partial(_ln_kernel, alpha=alpha),
        grid=(m // tm,),
        in_specs=[row, row, vec_b, vec, vec, vec_b, vec_b],
        out_specs=[row, row],
        out_shape=[jax.ShapeDtypeStruct((m, n), F32), jax.ShapeDtypeStruct((m, n), BF16)],
        compiler_params=_cparams(1, 2 * tm * n * (4 + 4 + 4 + 2) + 4 * tm * n * 4),
        name="residual_layernorm",
    )(h, mix, gate, ln_g.reshape(1, n), ln_b.reshape(1, n), sc, sh)


LN_ROW_BLOCK = 64


def _mm_ln_kernel(*refs, n_lhs, per_slab, alpha):
    x_refs = refs[:n_lhs]
    w_ref, h_ref, g_ref, lng_ref, lnb_ref, sc_ref, sh_ref, ho_ref, hm_ref = refs[n_lhs:]
    k = pl.program_id(1)

    @pl.when(k == 0)
    def _():
        ho_ref[...] = jnp.zeros_like(ho_ref)

    if n_lhs == 1:
        ho_ref[...] += jnp.dot(x_refs[0][...].astype(BF16), w_ref[...], preferred_element_type=F32)
    else:
        for idx in range(n_lhs):
            @pl.when(k // per_slab == idx)
            def _(idx=idx):
                ho_ref[...] += jnp.dot(x_refs[idx][...].astype(BF16), w_ref[...], preferred_element_type=F32)

    @pl.when(k == pl.num_programs(1) - 1)
    def _():
        gate = g_ref[0]
        scale = 1.0 + sc_ref[0]
        shift = sh_ref[0]

        def block(rb, c):
            rows = pl.ds(pl.multiple_of(rb * LN_ROW_BLOCK, LN_ROW_BLOCK), LN_ROW_BLOCK)
            r = alpha * h_ref[rows, :] + gate * ho_ref[rows, :]
            mu = jnp.mean(r, axis=-1, keepdims=True)
            rc = r - mu
            var = jnp.mean(rc * rc, axis=-1, keepdims=True)
            y = rc * lax.rsqrt(var + LN_EPS) * lng_ref[...] + lnb_ref[...]
            ho_ref[rows, :] = y
            hm_ref[rows, :] = (y * scale + shift).astype(BF16)
            return c

        lax.fori_loop(0, ho_ref.shape[0] // LN_ROW_BLOCK, block, 0)


def matmul_ln(xs, w, h, gate, ln_g, ln_b, sc, sh, *, alpha, rows_per_batch, tm=512, tk=512, name):
    slabs = list(xs) if isinstance(xs, (list, tuple)) else [xs]
    n_lhs = len(slabs)
    m, kslab = slabs[0].shape
    kk, n = w.shape
    tm, tk = min(tm, m, rows_per_batch), min(tk, kslab)
    assert rows_per_batch % tm == 0 and kslab % tk == 0 and tm % LN_ROW_BLOCK == 0
    per_slab = kslab // tk
    x_specs = [pl.BlockSpec((tm, tk), lambda i, k: (i, k % per_slab)) for _ in slabs]
    row = pl.BlockSpec((tm, n), lambda i, k: (i, 0))
    vec_b = pl.BlockSpec((1, 1, n), lambda i, k: ((i * tm) // rows_per_batch, 0, 0))
    vec = pl.BlockSpec((1, n), lambda i, k: (0, 0))
    xbytes = sum(tm * tk * s.dtype.itemsize for s in slabs)
    return pl.pallas_call(
        functools.partial(_mm_ln_kernel, n_lhs=n_lhs, per_slab=per_slab, alpha=alpha),
        grid=(m // tm, kk // tk),
        in_specs=x_specs + [pl.BlockSpec((tk, n), lambda i, k: (k, 0)), row, vec_b, vec, vec, vec_b, vec_b],
        out_specs=[row, row],
        out_shape=[jax.ShapeDtypeStruct((m, n), F32), jax.ShapeDtypeStruct((m, n), BF16)],
        compiler_params=_cparams(2, 2 * (xbytes + tk * n * 2 + tm * n * (4 + 4 + 2)) + 8 * LN_ROW_BLOCK * n * 4),
        name=name,
    )(*slabs, w, h, gate, ln_g.reshape(1, n), ln_b.reshape(1, n), sc, sh)


FNET_N2 = 128


def _dft_cos_sin(n):
    k = np.arange(n, dtype=np.int64)
    ang = (2.0 * np.pi / n) * ((k[:, None] * k[None, :]) % n).astype(np.float64)
    return np.cos(ang), np.sin(ang)


def _split_bf16(m):
    m = jnp.asarray(m, F32)
    hi = m.astype(BF16)
    return hi, (m - hi.astype(F32)).astype(BF16)


def _const_lhs3(m):
    hi, lo = _split_bf16(m)
    return jnp.concatenate([hi, hi, lo], axis=1)


def _const_rhs3(m):
    hi, lo = _split_bf16(m)
    return jnp.concatenate([hi, hi, lo], axis=0)


def _dot3_const_lhs(m3, x):
    xh = x.astype(BF16)
    xl = (x - xh.astype(F32)).astype(BF16)
    return jnp.dot(m3, jnp.concatenate([xh, xl, xh], axis=0), preferred_element_type=F32)


def _dot3_const_rhs(x, r3):
    xh = x.astype(BF16)
    xl = (x - xh.astype(F32)).astype(BF16)
    return jnp.dot(jnp.concatenate([xh, xl, xh], axis=1), r3, preferred_element_type=F32)


def _channel_dft(x, cs3_ref, hd):
    ps, qs = [], []
    for hh in range(x.shape[1] // hd):
        pq = _dot3_const_rhs(x[:, hh * hd:(hh + 1) * hd], cs3_ref[...])
        ps.append(pq[:, :hd])
        qs.append(pq[:, hd:])
    return jnp.concatenate(ps, axis=1), jnp.concatenate(qs, axis=1)


def _fnet_a_kernel(x_ref, cs3_ref, m1_ref, twr_ref, twi_ref, ar_ref, ai_ref, *, hd, n1):
    nsub = x_ref.shape[2]
    x = jnp.concatenate([x_ref[0, :, j, :] for j in range(nsub)], axis=0)
    p, q = _channel_dft(x, cs3_ref, hd)
    reps = x.shape[1] // V7X_LANES
    for j in range(nsub):
        rows = slice(j * n1, (j + 1) * n1)
        mp = _dot3_const_lhs(m1_ref[...], p[rows])
        mq = _dot3_const_lhs(m1_ref[...], q[rows])
        ar = mp[:n1] - mq[n1:]
        ai = -(mq[:n1] + mp[n1:])
        tr = jnp.tile(twr_ref[j], (1, reps))
        ti = jnp.tile(twi_ref[j], (1, reps))
        ar_ref[0, j] = ar * tr - ai * ti
        ai_ref[0, j] = ar * ti + ai * tr


def _fnet_c_kernel(ar_ref, ai_ref, m2_ref, o_ref, *, scale):
    for j in range(ar_ref.shape[2]):
        z = jnp.concatenate([ar_ref[0, :, j, :], ai_ref[0, :, j, :]], axis=0)
        o_ref[0, :, j, :] = _dot3_const_lhs(m2_ref[...], z) * scale


def _fnet_dense_kernel(x_ref, cs3_ref, ml_ref, o_ref, *, hd, scale):
    p, q = _channel_dft(x_ref[0], cs3_ref, hd)
    o_ref[0] = _dot3_const_lhs(ml_ref[...], jnp.concatenate([p, q], axis=0)) * scale


def fnet_mix(u, colblk, width):
    b, l, ctot = u.shape
    hd = width // FNET_HEADS
    scale = 1.0 / math.sqrt(l * hd)
    cd, sd = _dft_cos_sin(hd)
    cs3 = _const_rhs3(np.concatenate([cd, sd], axis=1))
    full = lambda shape: pl.BlockSpec(shape, lambda *_: (0,) * len(shape))
    n2 = FNET_N2
    if l < V7X_SUBLANES * n2:
        cl, sl = _dft_cos_sin(l)
        ml = _const_lhs3(np.concatenate([cl, -sl], axis=1))
        return pl.pallas_call(
            functools.partial(_fnet_dense_kernel, hd=hd, scale=scale),
            grid=(b,),
            in_specs=[pl.BlockSpec((1, l, width), lambda bb: (bb, 0, colblk)), full(cs3.shape), full(ml.shape)],
            out_specs=pl.BlockSpec((1, l, width), lambda bb: (bb, 0, 0)),
            out_shape=jax.ShapeDtypeStruct((b, l, width), F32),
            compiler_params=_cparams(1, 16 * l * width * 4),
            name="fnet_dense",
        )(u, cs3, ml)
    n1 = l // n2
    nsub = V7X_SUBLANES
    c1, s1 = _dft_cos_sin(n1)
    c2, s2 = _dft_cos_sin(n2)
    m1 = _const_lhs3(np.concatenate([c1, s1], axis=0))
    m2 = _const_lhs3(np.concatenate([c2, s2], axis=1))
    kk = (np.arange(n2, dtype=np.int64)[:, None] * np.arange(n1, dtype=np.int64)[None, :]) % l
    ang = (2.0 * np.pi / l) * kk.astype(np.float64)
    twr = jnp.asarray(np.broadcast_to(np.cos(ang)[:, :, None], (n2, n1, V7X_LANES)), F32)
    twi = jnp.asarray(np.broadcast_to(-np.sin(ang)[:, :, None], (n2, n1, V7X_LANES)), F32)
    ar, ai = pl.pallas_call(
        functools.partial(_fnet_a_kernel, hd=hd, n1=n1),
        grid=(b, n2 // nsub),
        in_specs=[pl.BlockSpec((1, n1, nsub, width), lambda bb, j: (bb, 0, j, colblk)),
                  full(cs3.shape), full(m1.shape),
                  pl.BlockSpec((nsub, n1, V7X_LANES), lambda bb, j: (j, 0, 0)),
                  pl.BlockSpec((nsub, n1, V7X_LANES), lambda bb, j: (j, 0, 0))],
        out_specs=[pl.BlockSpec((1, nsub, n1, width), lambda bb, j: (bb, j, 0, 0))] * 2,
        out_shape=[jax.ShapeDtypeStruct((b, n2, n1, width), F32)] * 2,
        compiler_params=_cparams(2, 40 * nsub * n1 * width * 4),
        name="fnet_stage_a",
    )(u.reshape(b, n1, n2, ctot), cs3, m1, twr, twi)
    y = pl.pallas_call(
        functools.partial(_fnet_c_kernel, scale=scale),
        grid=(b, n1 // nsub),
        in_specs=[pl.BlockSpec((1, n2, nsub, width), lambda bb, j: (bb, 0, j, 0)),
                  pl.BlockSpec((1, n2, nsub, width), lambda bb, j: (bb, 0, j, 0)),
                  full(m2.shape)],
        out_specs=pl.BlockSpec((1, n2, nsub, width), lambda bb, j: (bb, 0, j, 0)),
        out_shape=jax.ShapeDtypeStruct((b, n2, n1, width), F32),
        compiler_params=_cparams(2, 10 * n2 * nsub * width * 4),
        name="fnet_stage_c",
    )(ar, ai, m2)
    return y.reshape(b, l, width)


S5_LANE_BLOCK_GROUPS = V7X_LANES // S5_GROUP_CH
S5_STATE_BLOCK = S5_LANE_BLOCK_GROUPS * S5_STATE
S5_LEVELS = (1, 2, 4)
S5_SEGMENTS = V7X_SUBLANES


def _s5_direction_params(p, k, reverse, tseg):
    g, st = p["s5_lam_re"].shape[1:]
    lr, li = p["s5_lam_re"][k].astype(F32), p["s5_lam_im"][k].astype(F32)
    dt = jnp.exp(p["s5_log_dt"][k].astype(F32))[:, None]
    zr, zi = lr * dt, li * dt

    def power(s):
        e = jnp.exp(s * zr)
        return e * jnp.cos(s * zi), e * jnp.sin(s * zi)

    ar, ai = power(1.0)
    nr, ni = ar - 1.0, ai
    den = lr * lr + li * li
    cr, ci = (nr * lr + ni * li) / den, (ni * lr - nr * li) / den
    b_re, b_im = p["s5_b_re"].astype(F32), p["s5_b_im"].astype(F32)
    btr = b_re * cr[..., None] - b_im * ci[..., None]
    bti = b_re * ci[..., None] + b_im * cr[..., None]
    nj = g // S5_LANE_BLOCK_GROUPS
    eye = jnp.eye(S5_LANE_BLOCK_GROUPS, dtype=F32)

    def blockdiag_b(bt):
        b4 = bt.reshape(nj, S5_LANE_BLOCK_GROUPS, st, S5_GROUP_CH)
        return jnp.einsum("jgpc,gh->jgchp", b4, eye).reshape(nj, V7X_LANES, S5_STATE_BLOCK).astype(BF16)

    t = jnp.arange(V7X_SUBLANES)
    tabs_r, tabs_i = [], []
    for s in S5_LEVELS:
        pr, pi = power(float(s * tseg))
        keep = (t <= V7X_SUBLANES - 1 - s) if reverse else (t >= s)
        tabs_r.append(jnp.where(keep[:, None], pr.reshape(1, -1), 0.0))
        tabs_i.append(jnp.where(keep[:, None], pi.reshape(1, -1), 0.0))
    steps = ((V7X_SUBLANES - t) if reverse else (t + 1))[:, None, None].astype(F32) * float(tseg)
    e = jnp.exp(steps * zr[None])
    tabs_r.append((e * jnp.cos(steps * zi[None])).reshape(V7X_SUBLANES, -1))
    tabs_i.append((e * jnp.sin(steps * zi[None])).reshape(V7X_SUBLANES, -1))
    tabs_r.append(jnp.broadcast_to(ar.reshape(1, -1), (V7X_SUBLANES, ar.size)))
    tabs_i.append(jnp.broadcast_to(ai.reshape(1, -1), (V7X_SUBLANES, ai.size)))
    return blockdiag_b(btr), blockdiag_b(bti), jnp.stack(tabs_r), jnp.stack(tabs_i)


def _s5_output_params(p):
    g, ch, st = p["s5_c_re"].shape
    nj = g // S5_LANE_BLOCK_GROUPS
    eye = jnp.eye(S5_LANE_BLOCK_GROUPS, dtype=F32)

    def blockdiag_c(c):
        c4 = c.astype(F32).reshape(nj, S5_LANE_BLOCK_GROUPS, ch, st)
        return jnp.einsum("jgcp,gh->jgphc", c4, eye).reshape(nj, S5_STATE_BLOCK, V7X_LANES).astype(BF16)

    return blockdiag_c(p["s5_c_re"]), blockdiag_c(p["s5_c_im"])


def _s5_kernel(*refs, reverse, finish, tseg, nj):
    if finish:
        (u_ref, h0r_ref, h0i_ref, bre_ref, bim_ref, cre_ref, cim_ref, tr_ref, ti_ref,
         yf_ref, d_ref, wglu_ref, y_ref, hfr_ref, hfi_ref,
         up_s, xr_s, xi_s, car_s, cai_s, yp_s, yt_s) = refs
    else:
        (u_ref, h0r_ref, h0i_ref, bre_ref, bim_ref, cre_ref, cim_ref, tr_ref, ti_ref,
         y_ref, hfr_ref, hfi_ref, up_s, xr_s, xi_s, car_s, cai_s) = refs

    @pl.when(pl.program_id(1) == 0)
    def _():
        car_s[...] = h0r_ref[0]
        cai_s[...] = h0i_ref[0]

    nseg = S5_SEGMENTS
    sb = S5_STATE_BLOCK
    tile = lambda t: pl.ds(pl.multiple_of(t * nseg, nseg), nseg)

    def gather(t, c):
        up_s[tile(t), :] = u_ref[0, :, t, :]
        return c

    lax.fori_loop(0, tseg, gather, 0)
    edge = 0 if reverse else nseg - 1
    first = nseg - 1 if reverse else 0
    for j in range(nj):
        lanes = slice(j * V7X_LANES, (j + 1) * V7X_LANES)
        states = slice(j * sb, (j + 1) * sb)
        ub = up_s[:, lanes].astype(BF16)
        xr_s[...] = jnp.dot(ub, bre_ref[j], preferred_element_type=F32)
        xi_s[...] = jnp.dot(ub, bim_ref[j], preferred_element_type=F32)
        ar = tr_ref[len(S5_LEVELS) + 1, :, states]
        ai = ti_ref[len(S5_LEVELS) + 1, :, states]

        def step(t, carry, store, ar=ar, ai=ai):
            hr, hi = carry
            rows = tile((tseg - 1 - t) if reverse else t)
            hr, hi = ar * hr - ai * hi + xr_s[rows, :], ar * hi + ai * hr + xi_s[rows, :]
            if store:
                xr_s[rows, :] = hr
                xi_s[rows, :] = hi
            return hr, hi

        zero = jnp.zeros((nseg, sb), F32)
        gr, gi = lax.fori_loop(0, tseg, functools.partial(step, store=False), (zero, zero), unroll=4)
        for lvl, s in enumerate(S5_LEVELS):
            pr = tr_ref[lvl, :, states]
            pi = ti_ref[lvl, :, states]
            shift = (nseg - s) if reverse else s
            sr = pltpu.roll(gr, shift, 0)
            si = pltpu.roll(gi, shift, 0)
            gr, gi = gr + pr * sr - pi * si, gi + pr * si + pi * sr
        pr = tr_ref[len(S5_LEVELS), :, states]
        pi = ti_ref[len(S5_LEVELS), :, states]
        cr = jnp.broadcast_to(car_s[:, states], (nseg, sb))
        ci = jnp.broadcast_to(cai_s[:, states], (nseg, sb))
        gr, gi = gr + pr * cr - pi * ci, gi + pr * ci + pi * cr
        car_s[:, states] = gr[edge:edge + 1, :]
        cai_s[:, states] = gi[edge:edge + 1, :]
        row = lax.broadcasted_iota(jnp.int32, (nseg, sb), 0)
        shift = (nseg - 1) if reverse else 1
        start = (jnp.where(row == first, cr, pltpu.roll(gr, shift, 0)),
                 jnp.where(row == first, ci, pltpu.roll(gi, shift, 0)))
        lax.fori_loop(0, tseg, functools.partial(step, store=True), start, unroll=4)
        yp = (jnp.dot(xr_s[...].astype(BF16), cre_ref[j], preferred_element_type=F32)
              - jnp.dot(xi_s[...].astype(BF16), cim_ref[j], preferred_element_type=F32))
        if finish:
            yp_s[:, lanes] = yp + yf_ref[0, :, lanes]
        else:
            y_ref[0, :, lanes] = yp

    hfr_ref[0] = car_s[...]
    hfi_ref[0] = cai_s[...]
    if finish:
        def scatter(t, c):
            yt_s[:, t, :] = yp_s[tile(t), :]
            return c

        lax.fori_loop(0, tseg, scatter, 0)
        for s in range(nseg):
            y = yt_s[s] + d_ref[...] * u_ref[0, s]
            gl = _gelu_tanh(y)
            gate = jnp.dot(gl.astype(BF16), wglu_ref[...], preferred_element_type=F32)
            y_ref[0, s * tseg:(s + 1) * tseg, :] = (gl * jax.nn.sigmoid(gate)).astype(y_ref.dtype)


def s5_scan(u, colblk, width, dirp, outp, h0, *, reverse, tseg, fin=None):
    b, l, ctot = u.shape
    nseg = S5_SEGMENTS
    tt = nseg * tseg
    nc = l // tt
    nj = width // V7X_LANES
    ns = nj * S5_STATE_BLOCK
    bre, bim, tr, ti = dirp
    cre, cim = outp
    h0r, h0i = h0
    cmap = (lambda bb, c: (bb, nc - 1 - c, 0, colblk)) if reverse else (lambda bb, c: (bb, c, 0, colblk))
    omap = (lambda bb, c: (bb, nc - 1 - c, 0)) if reverse else (lambda bb, c: (bb, c, 0))
    full = lambda shape: pl.BlockSpec(shape, lambda *_: (0,) * len(shape))
    st_spec = pl.BlockSpec((1, 1, ns), lambda bb, c: (bb, 0, 0))
    in_specs = [pl.BlockSpec((1, nseg, tseg, width), cmap), st_spec, st_spec,
                full(bre.shape), full(bim.shape), full(cre.shape), full(cim.shape),
                full(tr.shape), full(ti.shape)]
    args = [u.reshape(b, l // tseg, tseg, ctot), h0r, h0i, bre, bim, cre, cim, tr, ti]
    scratch = [pltpu.VMEM((tt, width), F32),
               pltpu.VMEM((tt, S5_STATE_BLOCK), F32), pltpu.VMEM((tt, S5_STATE_BLOCK), F32),
               pltpu.VMEM((1, ns), F32), pltpu.VMEM((1, ns), F32)]
    vmem = 3 * tt * width * 4 + 2 * tt * S5_STATE_BLOCK * 4 + 2 * tt * width * 4
    if fin is not None:
        y_other, d, wglu = fin
        in_specs += [pl.BlockSpec((1, tt, width), omap), full((1, width)), full(wglu.shape)]
        args += [y_other, d.reshape(1, width).astype(F32), wglu]
        scratch += [pltpu.VMEM((tt, width), F32), pltpu.VMEM((nseg, tseg, width), F32)]
        vmem += 4 * tt * width * 4 + 2 * wglu.size * 2
    out_dtype = BF16 if fin is not None else F32
    vmem += 2 * 4 * bre.size * 2 + 2 * 2 * tr.size * 4
    return pl.pallas_call(
        functools.partial(_s5_kernel, reverse=reverse, finish=fin is not None, tseg=tseg, nj=nj),
        grid=(b, nc),
        in_specs=in_specs,
        out_specs=[pl.BlockSpec((1, tt, width), omap), st_spec, st_spec],
        out_shape=[jax.ShapeDtypeStruct((b, l, width), out_dtype),
                   jax.ShapeDtypeStruct((b, 1, ns), F32), jax.ShapeDtypeStruct((b, 1, ns), F32)],
        scratch_shapes=scratch,
        compiler_params=_cparams(2, vmem),
        name="s5_scan_bwd" if reverse else "s5_scan_fwd",
    )(*args)


def s5_mix(u, colblk, width, p, h0f, h0b, tseg=128):
    tseg = min(tseg, u.shape[1] // S5_SEGMENTS)
    outp = _s5_output_params(p)
    yf, hf_r, hf_i = s5_scan(u, colblk, width, _s5_direction_params(p, 0, False, tseg), outp, h0f,
                             reverse=False, tseg=tseg)
    y, hb_r, hb_i = s5_scan(u, colblk, width, _s5_direction_params(p, 1, True, tseg), outp, h0b,
                            reverse=True, tseg=tseg, fin=(yf, p["s5_d"], p["s5_w_glu"].astype(BF16)))
    return y, (hf_r, hf_i), (hb_r, hb_i)


def _fill_padded(pad_s, main, prev, nxt, halo):
    i = pl.program_id(1)
    n = pl.num_programs(1)
    tt = main.shape[0]
    pad_s[0:halo, :] = jnp.where(i > 0, prev, 0.0)
    pad_s[halo:halo + tt, :] = main
    pad_s[halo + tt:halo + tt + halo, :] = jnp.where(i < n - 1, nxt, 0.0)


def _dwconv_rows(pad_s, w_ref, width, halo, tt, row_block=32):
    off = halo - (width - 1) // 2
    outs = []
    for r0 in range(0, tt, row_block):
        rb = min(row_block, tt - r0)
        acc = None
        for k in range(width):
            term = w_ref[k:k + 1, :] * pad_s[r0 + off + k:r0 + off + k + rb, :]
            acc = term if acc is None else acc + term
        outs.append(acc)
    return outs


def _halo_specs(tt, halo, width, colblk, l):
    r = tt // halo
    nh = l // halo
    main = pl.BlockSpec((1, tt, width), lambda bb, i: (bb, i, colblk))
    prev = pl.BlockSpec((1, halo, width), lambda bb, i: (bb, jnp.maximum(i * r - 1, 0), colblk))
    nxt = pl.BlockSpec((1, halo, width), lambda bb, i: (bb, jnp.minimum((i + 1) * r, nh - 1), colblk))
    return [main, prev, nxt]


CV_HALO = 16


def _conv_module_kernel(v_ref, vp_ref, vn_ref, g_ref, gp_ref, gn_ref, w_ref, b_ref, lg_ref, lb_ref,
                        o_ref, pad_s, *, tt):
    glu = lambda v, g: v * jax.nn.sigmoid(g)
    _fill_padded(pad_s, glu(v_ref[0], g_ref[0]), glu(vp_ref[0], gp_ref[0]), glu(vn_ref[0], gn_ref[0]), CV_HALO)
    row_block = 32
    for bi, acc in enumerate(_dwconv_rows(pad_s, w_ref, CONV_WIDTH, CV_HALO, tt, row_block)):
        hcv = acc + b_ref[...]
        mu = jnp.mean(hcv, axis=-1, keepdims=True)
        hc = hcv - mu
        var = jnp.mean(hc * hc, axis=-1, keepdims=True)
        yn = hc * lax.rsqrt(var + LN_EPS) * lg_ref[...] + lb_ref[...]
        o_ref[0, bi * row_block:bi * row_block + acc.shape[0], :] = _silu(yn).astype(o_ref.dtype)


def conv_module(u, colblk_v, colblk_g, width, p, tt=128):
    b, l, _ = u.shape
    tt = min(tt, l)
    wk = jnp.zeros((32, width), F32).at[:CONV_WIDTH].set(p["cv_w"].astype(F32))
    vec = pl.BlockSpec((1, width), lambda bb, i: (0, 0))
    return pl.pallas_call(
        functools.partial(_conv_module_kernel, tt=tt),
        grid=(b, l // tt),
        in_specs=_halo_specs(tt, CV_HALO, width, colblk_v, l) + _halo_specs(tt, CV_HALO, width, colblk_g, l)
        + [pl.BlockSpec((32, width), lambda bb, i: (0, 0)), vec, vec, vec],
        out_specs=pl.BlockSpec((1, tt, width), lambda bb, i: (bb, i, 0)),
        out_shape=jax.ShapeDtypeStruct((b, l, width), BF16),
        scratch_shapes=[pltpu.VMEM((tt + 2 * CV_HALO, width), F32)],
        compiler_params=_cparams(2, 12 * (tt + 2 * CV_HALO) * width * 4),
        name="conv_module",
    )(u, u, u, u, u, u, wk, p["cv_b"].reshape(1, width).astype(F32),
      p["cv_ln_g"].reshape(1, width).astype(F32), p["cv_ln_b"].reshape(1, width).astype(F32))


SSD_HALO = 8
SSD_HEADS_PER_GROUP = 4
SSD_PAIR = V7X_LANES // SSD_HEAD_DIM


def _ssd_conv_kernel(x_ref, xp_ref, xn_ref, w_ref, b_ref, o_ref, pad_s, *, tt):
    _fill_padded(pad_s, x_ref[0], xp_ref[0], xn_ref[0], SSD_HALO)
    row_block = 64
    for bi, acc in enumerate(_dwconv_rows(pad_s, w_ref, SSD_CONV, SSD_HALO, tt, row_block)):
        o_ref[0, bi * row_block:bi * row_block + acc.shape[0], :] = _silu(acc + b_ref[...])


def ssd_conv(u, colblk0, ncol, width, conv_w, conv_b, tt=256):
    b, l, _ = u.shape
    tt = min(tt, l)
    r = tt // SSD_HALO
    nh = l // SSD_HALO
    wk = jnp.zeros((V7X_SUBLANES, ncol * width), F32).at[:SSD_CONV].set(conv_w.astype(F32))
    return pl.pallas_call(
        functools.partial(_ssd_conv_kernel, tt=tt),
        grid=(b, l // tt, ncol),
        in_specs=[pl.BlockSpec((1, tt, width), lambda bb, i, c: (bb, i, colblk0 + c)),
                  pl.BlockSpec((1, SSD_HALO, width), lambda bb, i, c: (bb, jnp.maximum(i * r - 1, 0), colblk0 + c)),
                  pl.BlockSpec((1, SSD_HALO, width),
                               lambda bb, i, c: (bb, jnp.minimum((i + 1) * r, nh - 1), colblk0 + c)),
                  pl.BlockSpec((V7X_SUBLANES, width), lambda bb, i, c: (0, c)),
                  pl.BlockSpec((1, width), lambda bb, i, c: (0, c))],
        out_specs=pl.BlockSpec((1, tt, width), lambda bb, i, c: (bb, i, c)),
        out_shape=jax.ShapeDtypeStruct((b, l, ncol * width), F32),
        scratch_shapes=[pltpu.VMEM((tt + 2 * SSD_HALO, width), F32)],
        compiler_params=_cparams(3, 8 * (tt + 2 * SSD_HALO) * width * 4),
        name="ssd_conv",
    )(u, u, u, wk, conv_b.reshape(1, -1).astype(F32))


def _ssd_kernel(*refs, reverse, finish, q, gw, nheads):
    if finish:
        (xbc_ref, dtr_ref, dtb_ref, arow_ref, s0_ref, yf_ref, z_ref, dvec_ref, nw_ref,
         y_ref, sfin_ref, st_s, yb_s) = refs
    else:
        xbc_ref, dtr_ref, dtb_ref, arow_ref, s0_ref, y_ref, sfin_ref, st_s = refs
        yb_s = y_ref.at[0]

    @pl.when(pl.program_id(1) == 0)
    def _():
        st_s[...] = s0_ref[0]

    n = SSD_STATE
    dt = _softplus(dtr_ref[0] + dtb_ref[...])
    dta = dt * arow_ref[...]
    row = lax.broadcasted_iota(jnp.int32, (q, q), 0)
    col = lax.broadcasted_iota(jnp.int32, (q, q), 1)
    tri = (row <= col) if reverse else (row >= col)
    cum = jnp.dot(tri.astype(F32), dta, precision=HIGHEST, preferred_element_type=F32)
    cum_t = cum.T
    dt_t = dt.T
    last = cum[0:1, :] if reverse else cum[q - 1:q, :]
    wdt = jnp.exp(last - cum) * dt
    ecum = jnp.exp(cum)
    elast = jnp.exp(last)
    lo_q = lax.broadcasted_iota(jnp.int32, (q, V7X_LANES), 1) < SSD_HEAD_DIM
    lo_n = lax.broadcasted_iota(jnp.int32, (n, V7X_LANES), 1) < SSD_HEAD_DIM
    b_off = nheads * SSD_HEAD_DIM
    c_off = b_off + SSD_NGROUPS * n
    for g in range(SSD_NGROUPS):
        bg = xbc_ref[0, :, b_off + g * n:b_off + (g + 1) * n]
        cb = xbc_ref[0, :, c_off + g * n:c_off + (g + 1) * n].astype(BF16)
        scores = lax.dot_general(cb, bg.astype(BF16), (((1,), (1,)), ((), ())), preferred_element_type=F32)
        for pr in range(SSD_HEADS_PER_GROUP // SSD_PAIR):
            pidx = g * (SSD_HEADS_PER_GROUP // SSD_PAIR) + pr
            lanes = slice(pidx * V7X_LANES, (pidx + 1) * V7X_LANES)
            xp = xbc_ref[0, :, lanes].astype(BF16)
            yd, stn = [], []
            for hh in range(SSD_PAIR):
                h = pidx * SSD_PAIR + hh
                seg = cum[:, h:h + 1] - cum_t[h:h + 1, :]
                dec = jnp.where(tri, jnp.exp(jnp.where(tri, seg, 0.0)), 0.0)
                w = (scores * dec * dt_t[h:h + 1, :]).astype(BF16)
                yd.append(jnp.dot(w, xp, preferred_element_type=F32))
                wb = (bg * wdt[:, h:h + 1]).T.astype(BF16)
                stn.append(jnp.dot(wb, xp, preferred_element_type=F32))
            h0 = pidx * SSD_PAIR
            s_prev = st_s[pidx]
            y_off = jnp.dot(cb, s_prev.astype(BF16), preferred_element_type=F32)
            y_off = y_off * jnp.where(lo_q, ecum[:, h0:h0 + 1], ecum[:, h0 + 1:h0 + 2])
            st_s[pidx] = (s_prev * jnp.where(lo_n, elast[:, h0:h0 + 1], elast[:, h0 + 1:h0 + 2])
                          + jnp.where(lo_n, stn[0], stn[1]))
            yb_s[:, lanes] = jnp.where(lo_q, yd[0], yd[1]) + y_off
    sfin_ref[0] = st_s[...]
    if finish:
        xs = xbc_ref[0, :, :b_off]
        y = yf_ref[0] + yb_s[...] + dvec_ref[...] * xs
        t = y * _silu(z_ref[0])
        for g in range(SSD_NGROUPS):
            tg = t[:, g * gw:(g + 1) * gw]
            ms = jnp.mean(tg * tg, axis=-1, keepdims=True)
            y_ref[0, :, g * gw:(g + 1) * gw] = (tg * lax.rsqrt(ms + RMS_EPS)
                                                * nw_ref[:, g * gw:(g + 1) * gw]).astype(y_ref.dtype)


def ssd_scan(xbc, dtraw, dcol, dt_bias, a_head, s0, *, reverse, fin=None, q=256):
    b, l, cw = xbc.shape
    q = min(q, l)
    nc = l // q
    nheads = dt_bias.shape[0]
    width = nheads * SSD_HEAD_DIM
    npair = nheads // SSD_PAIR
    pad = lambda v: jnp.zeros((1, V7X_LANES), F32).at[0, :nheads].set(v.astype(F32))
    cmap = (lambda bb, c: (bb, nc - 1 - c, 0)) if reverse else (lambda bb, c: (bb, c, 0))
    dmap = (lambda bb, c: (bb, nc - 1 - c, dcol)) if reverse else (lambda bb, c: (bb, c, dcol))
    full = lambda shape: pl.BlockSpec(shape, lambda *_: (0,) * len(shape))
    st_spec = pl.BlockSpec((1, npair, SSD_STATE, V7X_LANES), lambda bb, c: (bb, 0, 0, 0))
    in_specs = [pl.BlockSpec((1, q, cw), cmap), pl.BlockSpec((1, q, V7X_LANES), dmap),
                full((1, V7X_LANES)), full((1, V7X_LANES)), st_spec]
    args = [xbc, dtraw, pad(dt_bias), pad(a_head), s0]
    scratch = [pltpu.VMEM((npair, SSD_STATE, V7X_LANES), F32)]
    if fin is not None:
        y_other, z, zcol, dvec, nw = fin
        zmap = (lambda bb, c: (bb, nc - 1 - c, zcol)) if reverse else (lambda bb, c: (bb, c, zcol))
        in_specs += [pl.BlockSpec((1, q, width), cmap), pl.BlockSpec((1, q, width), zmap),
                     full((1, width)), full((1, width))]
        args += [y_other, z, dvec, nw]
        scratch.append(pltpu.VMEM((q, width), F32))
    out_dtype = BF16 if fin is not None else F32
    return pl.pallas_call(
        functools.partial(_ssd_kernel, reverse=reverse, finish=fin is not None, q=q,
                          gw=width // SSD_NGROUPS, nheads=nheads),
        grid=(b, nc),
        in_specs=in_specs,
        out_specs=[pl.BlockSpec((1, q, width), cmap), st_spec],
        out_shape=[jax.ShapeDtypeStruct((b, l, width), out_dtype),
                   jax.ShapeDtypeStruct((b, npair, SSD_STATE, V7X_LANES), F32)],
        scratch_shapes=scratch,
        compiler_params=_cparams(2, 4 * q * cw * 4 + 12 * q * width * 4 + 24 * q * q * 4),
        name="ssd_scan_bwd" if reverse else "ssd_scan_fwd",
    )(*args)


def ssd_mix(u, dtraw, zcol, xcol0, width, p, s0f, s0b):
    nheads = p["ssd_dt_bias"].shape[1]
    xbc = ssd_conv(u, xcol0, 2, width, p["ssd_conv_w"], p["ssd_conv_b"])
    a_head = -jnp.exp(p["ssd_a_log"].astype(F32))
    yf, sf = ssd_scan(xbc, dtraw, 0, p["ssd_dt_bias"][0], a_head[0], s0f, reverse=False)
    dvec = jnp.repeat(p["ssd_d"].astype(F32), SSD_HEAD_DIM).reshape(1, width)
    y, sb = ssd_scan(xbc, dtraw, 1, p["ssd_dt_bias"][1], a_head[1], s0b, reverse=True,
                     fin=(yf, u, zcol, dvec, p["ssd_norm_w"].reshape(1, width).astype(F32)))
    return y, sf, sb


def _bf16_weights(w_in, w_out, w_gate, w_up, w_down, d, nheads, hidden_pad):
    gw = d // N_MIXERS
    n_main = 4 * gw + gw + (gw + 2 * SSD_NGROUPS * SSD_STATE)
    hidden = w_gate.shape[2]
    pad_last = lambda w, n: jnp.pad(w, ((0, 0), (0, 0), (0, n - w.shape[2])))
    w_dt = jnp.concatenate([pad_last(w_in[:, :, n_main:n_main + nheads], V7X_LANES),
                            pad_last(w_in[:, :, n_main + nheads:], V7X_LANES)], axis=2)
    return dict(
        w_main=w_in[:, :, :n_main].astype(BF16), w_dt=w_dt.astype(BF16), w_out=w_out.astype(BF16),
        w_gate=pad_last(w_gate, hidden_pad).astype(BF16),
        w_up=pad_last(w_up, hidden_pad).astype(BF16),
        w_down=jnp.pad(w_down, ((0, 0), (0, hidden_pad - hidden), (0, 0))).astype(BF16))


def _mixers(hm, wts, p, s5_h0, ssd_s0, want_output=True):
    b, l, d = hm.shape
    gw = d // N_MIXERS
    hm2 = hm.reshape(b * l, d)
    tm = 1024 if b * l >= 1024 else b * l
    u = matmul(hm2, wts["w_main"], tm=tm, tn=1024, name="w_in_main").reshape(b, l, -1)
    dtraw = matmul(hm2, wts["w_dt"], tm=tm, tn=2 * V7X_LANES, name="w_in_dt").reshape(b, l, -1)
    ys, s5f, s5b = s5_mix(u, 1, gw, p, *s5_h0)
    yd, sdf, sdb = ssd_mix(u, dtraw, 4, 5, gw, p, *ssd_s0)
    states = ((s5f, s5b), (sdf, sdb))
    if not want_output:
        return None, states
    ya = fnet_mix(u, 0, gw)
    yc = conv_module(u, 2, 3, gw, p)
    return [y.reshape(b * l, gw) for y in (ya, ys, yc, yd)], states


def _layer_tail(h, mix, wts, mod, mod_next, ln, *, alpha, rows_per_batch):
    (ln1_g, ln1_b, ln2_g, ln2_b) = ln
    h, hm = matmul_ln(mix, wts["w_out"], h, mod[2], ln1_g, ln1_b, mod[4], mod[3], alpha=alpha,
                      rows_per_batch=rows_per_batch, name="w_out_ln1")
    hid = gate_up(hm, wts["w_gate"], wts["w_up"], tm=1024, tn=512)
    ff = matmul_ktiled(hid, wts["w_down"], tm=512, tk=1024, name="w_down")
    return residual_layernorm(h, ff, mod[5], ln2_g, ln2_b, mod_next[1], mod_next[0], alpha=alpha,
                              rows_per_batch=rows_per_batch)


def kernel(x, c, ctx, c_ctx, w_ada, b_ada, w_in, s5_lam_re, s5_lam_im, s5_log_dt, s5_b_re, s5_b_im,
           s5_c_re, s5_c_im, s5_d, s5_w_glu, cv_w, cv_b, cv_ln_g, cv_ln_b, ssd_conv_w, ssd_conv_b,
           ssd_a_log, ssd_dt_bias, ssd_d, ssd_norm_w, w_out, ln1_g, ln1_b, w_gate, w_up, w_down,
           ln2_g, ln2_b):
    bsz, seq, d = x.shape
    cl = ctx.shape[1]
    depth = w_in.shape[0]
    gw = d // N_MIXERS
    alpha = (2 * depth) ** 0.25
    hidden = w_gate.shape[2]
    hidden_pad = -(-hidden // 1024) * 1024
    per_layer = dict(s5_lam_re=s5_lam_re, s5_lam_im=s5_lam_im, s5_log_dt=s5_log_dt,
                     s5_b_re=s5_b_re, s5_b_im=s5_b_im, s5_c_re=s5_c_re, s5_c_im=s5_c_im, s5_d=s5_d,
                     s5_w_glu=s5_w_glu, cv_w=cv_w, cv_b=cv_b, cv_ln_g=cv_ln_g, cv_ln_b=cv_ln_b,
                     ssd_conv_w=ssd_conv_w, ssd_conv_b=ssd_conv_b, ssd_a_log=ssd_a_log,
                     ssd_dt_bias=ssd_dt_bias, ssd_d=ssd_d, ssd_norm_w=ssd_norm_w)

    c8 = jnp.zeros((V7X_SUBLANES, d), F32).at[:bsz].set(c).at[bsz].set(c_ctx)
    mods = ada_modulation(c8, w_ada, b_ada)

    def mod_vectors(i):
        m = mods[i].reshape(V7X_SUBLANES, 6, d)
        lat = [m[:bsz, k].reshape(bsz, 1, d) for k in range(6)]
        cx = [jnp.broadcast_to(m[bsz, k].reshape(1, 1, d), (bsz, 1, d)) for k in range(6)]
        return lat, cx

    nstate = (gw // V7X_LANES) * S5_STATE_BLOCK
    npair = ssd_dt_bias.shape[2] // SSD_PAIR
    z_s5 = (jnp.zeros((bsz, 1, nstate), F32), jnp.zeros((bsz, 1, nstate), F32))
    z_ssd = jnp.zeros((bsz, npair, SSD_STATE, V7X_LANES), F32)

    wts_all = _bf16_weights(w_in, w_out, w_gate, w_up, w_down, d, ssd_dt_bias.shape[2], hidden_pad)
    lat0, cx0 = mod_vectors(0)
    h, hm = prologue(x, lat0[1], lat0[0])
    h = h.reshape(bsz * seq, d)
    hc = ctx.reshape(bsz * cl, d)
    hcm = modulate(ctx, cx0[1], cx0[0])
    for i in range(depth):
        with_ctx = i < depth - 1
        p = {k: v[i] for k, v in per_layer.items()}
        wts = {k: v[i] for k, v in wts_all.items()}
        lat, cx = mod_vectors(i)
        lat_next, cx_next = mod_vectors(min(i + 1, depth - 1))
        ln = (ln1_g[i], ln1_b[i], ln2_g[i], ln2_b[i])
        mix_c, ((c5f, c5b), (cdf, cdb)) = _mixers(hcm.reshape(bsz, cl, d), wts, p, (z_s5, z_s5),
                                                   (z_ssd, z_ssd), want_output=with_ctx)
        mix_l, _ = _mixers(hm.reshape(bsz, seq, d), wts, p, (c5f, c5b), (cdf, cdb))
        h, hm = _layer_tail(h, mix_l, wts, lat, lat_next, ln, alpha=alpha, rows_per_batch=seq)
        if with_ctx:
            hc, hcm = _layer_tail(hc, mix_c, wts, cx, cx_next, ln, alpha=alpha, rows_per_batch=cl)
    return h.reshape(bsz, seq, d).astype(x.dtype)
```

```python
import functools
import math

import jax
import jax.numpy as jnp
import numpy as np
from jax import lax
from jax.experimental import pallas as pl
from jax.experimental.pallas import tpu as pltpu

F32 = jnp.float32
BF16 = jnp.bfloat16
HIGHEST = lax.Precision.HIGHEST

GRID_W = 64
N_MIXERS = 4
FNET_HEADS = 4
S5_GROUP_CH = 16
S5_STATE = 64
CONV_WIDTH = 31
SSD_HEAD_DIM = 64
SSD_NGROUPS = 4
SSD_STATE = 128
SSD_CONV = 5
LN_EPS = 1e-5
RMS_EPS = 1e-5

V7X_LANES = 128
V7X_SUBLANES = 8
V7X_VMEM_LIMIT_BYTES = 60000 * 1024
FFN_TILE = 512


def _cparams(n_axes, vmem_bytes):
    limit = int(min(max(vmem_bytes * 5 // 4 + (2 << 20), 16 << 20), V7X_VMEM_LIMIT_BYTES))
    return pltpu.CompilerParams(dimension_semantics=("arbitrary",) * n_axes,
                                vmem_limit_bytes=limit)


def _silu(x):
    return x * jax.nn.sigmoid(x)


def _gelu_tanh(x):
    return 0.5 * x * (1.0 + jnp.tanh(math.sqrt(2.0 / math.pi) * (x + 0.044715 * (x * x * x))))


def _softplus(x):
    return jnp.maximum(x, 0.0) + jnp.log1p(jnp.exp(-jnp.abs(x)))


def _ada_kernel(c_ref, w_ref, b_ref, o_ref):
    cs = _silu(c_ref[...])
    o_ref[0] = jnp.dot(cs, w_ref[0], precision=HIGHEST, preferred_element_type=F32) + b_ref[0]


def ada_modulation(c8, w_ada, b_ada, tn=1024):
    depth, d, n = w_ada.shape
    tn = min(tn, n)
    return pl.pallas_call(
        _ada_kernel,
        grid=(depth, n // tn),
        in_specs=[pl.BlockSpec((V7X_SUBLANES, d), lambda l, j: (0, 0)),
                  pl.BlockSpec((1, d, tn), lambda l, j: (l, 0, j)),
                  pl.BlockSpec((1, 1, tn), lambda l, j: (l, 0, j))],
        out_specs=pl.BlockSpec((1, V7X_SUBLANES, tn), lambda l, j: (l, 0, j)),
        out_shape=jax.ShapeDtypeStruct((depth, V7X_SUBLANES, n), F32),
        compiler_params=_cparams(2, 2 * d * tn * 4 + 6 * d * tn * 2),
        name="ada_modulation",
    )(c8, w_ada, b_ada.reshape(depth, 1, n))


def _prologue_kernel(x_ref, tr_ref, tc_ref, sc_ref, sh_ref, h_ref, hm_ref, *, tp, half):
    sc = 1.0 + sc_ref[0]
    sh = sh_ref[0]
    for rr in range(tp // GRID_W):
        rows = slice(rr * GRID_W, (rr + 1) * GRID_W)
        lo = x_ref[0, rows, :half] + tr_ref[rr:rr + 1, :]
        hi = x_ref[0, rows, half:] + tc_ref[...]
        h_ref[0, rows, :half] = lo
        h_ref[0, rows, half:] = hi
        hm_ref[0, rows, :half] = (lo * sc[:, :half] + sh[:, :half]).astype(BF16)
        hm_ref[0, rows, half:] = (hi * sc[:, half:] + sh[:, half:]).astype(BF16)


def prologue(x, sc, sh, tp=512):
    b, l, d = x.shape
    tp = min(tp, l)
    half = d // 2
    q = d // 4
    omega = jnp.exp(-math.log(10000.0) * jnp.arange(q, dtype=F32) / q)[None]
    r = jnp.arange(l // GRID_W, dtype=F32)[:, None]
    col = jnp.arange(GRID_W, dtype=F32)[:, None]
    tab_r = jnp.concatenate([jnp.sin(r * omega), jnp.cos(r * omega)], -1)
    tab_c = jnp.concatenate([jnp.sin(col * omega), jnp.cos(col * omega)], -1)
    rpt = tp // GRID_W
    return pl.pallas_call(
        functools.partial(_prologue_kernel, tp=tp, half=half),
        grid=(b, l // tp),
        in_specs=[pl.BlockSpec((1, tp, d), lambda bb, i: (bb, i, 0)),
                  pl.BlockSpec((rpt, half), lambda bb, i: (i, 0)),
                  pl.BlockSpec((GRID_W, half), lambda bb, i: (0, 0)),
                  pl.BlockSpec((1, 1, d), lambda bb, i: (bb, 0, 0)),
                  pl.BlockSpec((1, 1, d), lambda bb, i: (bb, 0, 0))],
        out_specs=[pl.BlockSpec((1, tp, d), lambda bb, i: (bb, i, 0)),
                   pl.BlockSpec((1, tp, d), lambda bb, i: (bb, i, 0))],
        out_shape=[jax.ShapeDtypeStruct((b, l, d), F32), jax.ShapeDtypeStruct((b, l, d), BF16)],
        compiler_params=_cparams(2, 2 * tp * d * (4 + 4 + 2)),
        name="prologue",
    )(x, tab_r, tab_c, sc, sh)


def _modulate_kernel(x_ref, sc_ref, sh_ref, o_ref):
    o_ref[0] = (x_ref[0] * (1.0 + sc_ref[0]) + sh_ref[0]).astype(BF16)


def modulate(x, sc, sh, tp=256):
    b, l, d = x.shape
    tp = min(tp, l)
    return pl.pallas_call(
        _modulate_kernel,
        grid=(b, l // tp),
        in_specs=[pl.BlockSpec((1, tp, d), lambda bb, i: (bb, i, 0)),
                  pl.BlockSpec((1, 1, d), lambda bb, i: (bb, 0, 0)),
                  pl.BlockSpec((1, 1, d), lambda bb, i: (bb, 0, 0))],
        out_specs=pl.BlockSpec((1, tp, d), lambda bb, i: (bb, i, 0)),
        out_shape=jax.ShapeDtypeStruct((b, l, d), BF16),
        compiler_params=_cparams(2, 2 * tp * d * 6),
        name="modulate",
    )(x, sc, sh)


def _mm_kernel(*refs, n_lhs, n_w, cast_w):
    x_refs = refs[:n_lhs]
    w_refs = refs[n_lhs:n_lhs + n_w]
    o_ref = refs[n_lhs + n_w]
    if cast_w:
        wb = refs[n_lhs + n_w + 1:]

        @pl.when(pl.program_id(1) == 0)
        def _():
            for w_ref, s in zip(w_refs, wb):
                s[...] = w_ref[...].astype(BF16)

        ws = [s[...] for s in wb]
    else:
        ws = [w_ref[...] for w_ref in w_refs]
    xs = [x_ref[...].astype(BF16) for x_ref in x_refs]
    x = xs[0] if n_lhs == 1 else jnp.concatenate(xs, axis=1)
    acc = [jnp.dot(x, w, preferred_element_type=F32) for w in ws]
    y = acc[0] if n_w == 1 else _silu(acc[0]) * acc[1]
    o_ref[...] = y.astype(o_ref.dtype)


def matmul(xs, ws, *, layer=None, n_out=None, tm=1024, tn=512, out_dtype=F32, name="matmul"):
    slabs = list(xs) if isinstance(xs, (list, tuple)) else [xs]
    wl = list(ws) if isinstance(ws, (list, tuple)) else [ws]
    m = slabs[0].shape[0]
    kk, n = wl[0].shape[-2:]
    n = n if n_out is None else n_out
    tm, tn = min(tm, m), min(tn, n)
    assert m % tm == 0 and n % tn == 0
    cast_w = layer is not None
    x_specs = [pl.BlockSpec((tm, s.shape[1]), lambda j, i: (i, 0)) for s in slabs]
    if cast_w:
        w_specs = [pl.BlockSpec((None, kk, tn), lambda j, i: (layer, 0, j), pipeline_mode=pl.Buffered(1))
                   for _ in wl]
        scratch = [pltpu.VMEM((kk, tn), BF16) for _ in wl]
    else:
        w_specs = [pl.BlockSpec((kk, tn), lambda j, i: (0, j)) for _ in wl]
        scratch = []
    ob = jnp.dtype(out_dtype).itemsize
    xbytes = sum(tm * s.shape[1] * s.dtype.itemsize for s in slabs)
    wbytes = len(wl) * kk * tn * (4 + 2 if cast_w else 2 * 2)
    return pl.pallas_call(
        functools.partial(_mm_kernel, n_lhs=len(slabs), n_w=len(wl), cast_w=cast_w),
        grid=(n // tn, m // tm),
        in_specs=x_specs + w_specs,
        out_specs=pl.BlockSpec((tm, tn), lambda j, i: (i, j)),
        out_shape=jax.ShapeDtypeStruct((m, n), out_dtype),
        scratch_shapes=scratch,
        compiler_params=_cparams(2, 2 * xbytes + wbytes + 2 * tm * tn * ob + (len(wl) + 1) * tm * tn * 4
                                 + (tm * kk * 2 if len(slabs) > 1 else 0)),
        name=name,
    )(*slabs, *wl)


def _ln_kernel(h_ref, mix_ref, g_ref, lng_ref, lnb_ref, sc_ref, sh_ref, ho_ref, hm_ref, *, alpha):
    r = alpha * h_ref[...] + g_ref[0] * mix_ref[...]
    mu = jnp.mean(r, axis=-1, keepdims=True)
    rc = r - mu
    var = jnp.mean(rc * rc, axis=-1, keepdims=True)
    y = rc * lax.rsqrt(var + LN_EPS) * lng_ref[...] + lnb_ref[...]
    ho_ref[...] = y
    hm_ref[...] = (y * (1.0 + sc_ref[0]) + sh_ref[0]).astype(BF16)


def residual_layernorm(h, mix, gate, ln_g, ln_b, sc, sh, *, alpha, rows_per_batch, tm=256):
    m, n = h.shape
    tm = min(tm, m, rows_per_batch)
    row = pl.BlockSpec((tm, n), lambda i: (i, 0))
    vec_b = pl.BlockSpec((1, 1, n), lambda i: ((i * tm) // rows_per_batch, 0, 0))
    vec = pl.BlockSpec((1, n), lambda i: (0, 0))
    return pl.pallas_call(
        functools.partial(_ln_kernel, alpha=alpha),
        grid=(m // tm,),
        in_specs=[row, row, vec_b, vec, vec, vec_b, vec_b],
        out_specs=[row, row],
        out_shape=[jax.ShapeDtypeStruct((m, n), F32), jax.ShapeDtypeStruct((m, n), BF16)],
        compiler_params=_cparams(1, 2 * tm * n * (4 + 4 + 4 + 2) + 4 * tm * n * 4),
        name="residual_layernorm",
    )(h, mix, gate, ln_g.reshape(1, n), ln_b.reshape(1, n), sc, sh)


FNET_N2 = 128


def _dft_cos_sin(n):
    k = np.arange(n, dtype=np.int64)
    ang = (2.0 * np.pi / n) * ((k[:, None] * k[None, :]) % n).astype(np.float64)
    return np.cos(ang), np.sin(ang)


def _split_bf16(m):
    m = jnp.asarray(m, F32)
    hi = m.astype(BF16)
    return hi, (m - hi.astype(F32)).astype(BF16)


def _const_lhs3(m):
    hi, lo = _split_bf16(m)
    return jnp.concatenate([hi, hi, lo], axis=1)


def _const_rhs3(m):
    hi, lo = _split_bf16(m)
    return jnp.concatenate([hi, hi, lo], axis=0)


def _dot3_const_lhs(m3, x):
    xh = x.astype(BF16)
    xl = (x - xh.astype(F32)).astype(BF16)
    return jnp.dot(m3, jnp.concatenate([xh, xl, xh], axis=0), preferred_element_type=F32)


def _dot3_const_rhs(x, r3):
    xh = x.astype(BF16)
    xl = (x - xh.astype(F32)).astype(BF16)
    return jnp.dot(jnp.concatenate([xh, xl, xh], axis=1), r3, preferred_element_type=F32)


def _channel_dft(x, cs3_ref, hd):
    ps, qs = [], []
    for hh in range(x.shape[1] // hd):
        pq = _dot3_const_rhs(x[:, hh * hd:(hh + 1) * hd], cs3_ref[...])
        ps.append(pq[:, :hd])
        qs.append(pq[:, hd:])
    return jnp.concatenate(ps, axis=1), jnp.concatenate(qs, axis=1)


def _fnet_a_kernel(x_ref, cs3_ref, m1_ref, twr_ref, twi_ref, ar_ref, ai_ref, *, hd, n1):
    nsub = x_ref.shape[2]
    x = jnp.concatenate([x_ref[0, :, j, :] for j in range(nsub)], axis=0)
    p, q = _channel_dft(x, cs3_ref, hd)
    reps = x.shape[1] // V7X_LANES
    for j in range(nsub):
        rows = slice(j * n1, (j + 1) * n1)
        mp = _dot3_const_lhs(m1_ref[...], p[rows])
        mq = _dot3_const_lhs(m1_ref[...], q[rows])
        ar = mp[:n1] - mq[n1:]
        ai = -(mq[:n1] + mp[n1:])
        tr = jnp.tile(twr_ref[j], (1, reps))
        ti = jnp.tile(twi_ref[j], (1, reps))
        ar_ref[0, j] = ar * tr - ai * ti
        ai_ref[0, j] = ar * ti + ai * tr


def _fnet_c_kernel(ar_ref, ai_ref, m2_ref, o_ref, *, scale):
    for j in range(ar_ref.shape[2]):
        z = jnp.concatenate([ar_ref[0, :, j, :], ai_ref[0, :, j, :]], axis=0)
        o_ref[0, :, j, :] = _dot3_const_lhs(m2_ref[...], z) * scale


def _fnet_dense_kernel(x_ref, cs3_ref, ml_ref, o_ref, *, hd, scale):
    p, q = _channel_dft(x_ref[0], cs3_ref, hd)
    o_ref[0] = _dot3_const_lhs(ml_ref[...], jnp.concatenate([p, q], axis=0)) * scale


def fnet_mix(u, colblk, width):
    b, l, ctot = u.shape
    hd = width // FNET_HEADS
    scale = 1.0 / math.sqrt(l * hd)
    cd, sd = _dft_cos_sin(hd)
    cs3 = _const_rhs3(np.concatenate([cd, sd], axis=1))
    full = lambda shape: pl.BlockSpec(shape, lambda *_: (0,) * len(shape))
    n2 = FNET_N2
    if l < V7X_SUBLANES * n2:
        cl, sl = _dft_cos_sin(l)
        ml = _const_lhs3(np.concatenate([cl, -sl], axis=1))
        return pl.pallas_call(
            functools.partial(_fnet_dense_kernel, hd=hd, scale=scale),
            grid=(b,),
            in_specs=[pl.BlockSpec((1, l, width), lambda bb: (bb, 0, colblk)), full(cs3.shape), full(ml.shape)],
            out_specs=pl.BlockSpec((1, l, width), lambda bb: (bb, 0, 0)),
            out_shape=jax.ShapeDtypeStruct((b, l, width), F32),
            compiler_params=_cparams(1, 16 * l * width * 4),
            name="fnet_dense",
        )(u, cs3, ml)
    n1 = l // n2
    nsub = V7X_SUBLANES
    c1, s1 = _dft_cos_sin(n1)
    c2, s2 = _dft_cos_sin(n2)
    m1 = _const_lhs3(np.concatenate([c1, s1], axis=0))
    m2 = _const_lhs3(np.concatenate([c2, s2], axis=1))
    kk = (np.arange(n2, dtype=np.int64)[:, None] * np.arange(n1, dtype=np.int64)[None, :]) % l
    ang = (2.0 * np.pi / l) * kk.astype(np.float64)
    twr = jnp.asarray(np.broadcast_to(np.cos(ang)[:, :, None], (n2, n1, V7X_LANES)), F32)
    twi = jnp.asarray(np.broadcast_to(-np.sin(ang)[:, :, None], (n2, n1, V7X_LANES)), F32)
    ar, ai = pl.pallas_call(
        functools.partial(_fnet_a_kernel, hd=hd, n1=n1),
        grid=(b, n2 // nsub),
        in_specs=[pl.BlockSpec((1, n1, nsub, width), lambda bb, j: (bb, 0, j, colblk)),
                  full(cs3.shape), full(m1.shape),
                  pl.BlockSpec((nsub, n1, V7X_LANES), lambda bb, j: (j, 0, 0)),
                  pl.BlockSpec((nsub, n1, V7X_LANES), lambda bb, j: (j, 0, 0))],
        out_specs=[pl.BlockSpec((1, nsub, n1, width), lambda bb, j: (bb, j, 0, 0))] * 2,
        out_shape=[jax.ShapeDtypeStruct((b, n2, n1, width), F32)] * 2,
        compiler_params=_cparams(2, 40 * nsub * n1 * width * 4),
        name="fnet_stage_a",
    )(u.reshape(b, n1, n2, ctot), cs3, m1, twr, twi)
    y = pl.pallas_call(
        functools.partial(_fnet_c_kernel, scale=scale),
        grid=(b, n1 // nsub),
        in_specs=[pl.BlockSpec((1, n2, nsub, width), lambda bb, j: (bb, 0, j, 0)),
                  pl.BlockSpec((1, n2, nsub, width), lambda bb, j: (bb, 0, j, 0)),
                  full(m2.shape)],
        out_specs=pl.BlockSpec((1, n2, nsub, width), lambda bb, j: (bb, 0, j, 0)),
        out_shape=jax.ShapeDtypeStruct((b, n2, n1, width), F32),
        compiler_params=_cparams(2, 10 * n2 * nsub * width * 4),
        name="fnet_stage_c",
    )(ar, ai, m2)
    return y.reshape(b, l, width)


S5_LANE_BLOCK_GROUPS = V7X_LANES // S5_GROUP_CH
S5_STATE_BLOCK = S5_LANE_BLOCK_GROUPS * S5_STATE
S5_LEVELS = (1, 2, 4)
S5_SEGMENTS = V7X_SUBLANES


def _s5_direction_params(p, k, reverse, tseg):
    g, st = p["s5_lam_re"].shape[1:]
    lr, li = p["s5_lam_re"][k].astype(F32), p["s5_lam_im"][k].astype(F32)
    dt = jnp.exp(p["s5_log_dt"][k].astype(F32))[:, None]
    zr, zi = lr * dt, li * dt

    def power(s):
        e = jnp.exp(s * zr)
        return e * jnp.cos(s * zi), e * jnp.sin(s * zi)

    ar, ai = power(1.0)
    nr, ni = ar - 1.0, ai
    den = lr * lr + li * li
    cr, ci = (nr * lr + ni * li) / den, (ni * lr - nr * li) / den
    b_re, b_im = p["s5_b_re"].astype(F32), p["s5_b_im"].astype(F32)
    btr = b_re * cr[..., None] - b_im * ci[..., None]
    bti = b_re * ci[..., None] + b_im * cr[..., None]
    nj = g // S5_LANE_BLOCK_GROUPS
    eye = jnp.eye(S5_LANE_BLOCK_GROUPS, dtype=F32)

    def blockdiag_b(bt):
        b4 = bt.reshape(nj, S5_LANE_BLOCK_GROUPS, st, S5_GROUP_CH)
        return jnp.einsum("jgpc,gh->jgchp", b4, eye).reshape(nj, V7X_LANES, S5_STATE_BLOCK).astype(BF16)

    t = jnp.arange(V7X_SUBLANES)
    tabs_r, tabs_i = [], []
    for s in S5_LEVELS:
        pr, pi = power(float(s * tseg))
        keep = (t <= V7X_SUBLANES - 1 - s) if reverse else (t >= s)
        tabs_r.append(jnp.where(keep[:, None], pr.reshape(1, -1), 0.0))
        tabs_i.append(jnp.where(keep[:, None], pi.reshape(1, -1), 0.0))
    steps = ((V7X_SUBLANES - t) if reverse else (t + 1))[:, None, None].astype(F32) * float(tseg)
    e = jnp.exp(steps * zr[None])
    tabs_r.append((e * jnp.cos(steps * zi[None])).reshape(V7X_SUBLANES, -1))
    tabs_i.append((e * jnp.sin(steps * zi[None])).reshape(V7X_SUBLANES, -1))
    tabs_r.append(jnp.broadcast_to(ar.reshape(1, -1), (V7X_SUBLANES, ar.size)))
    tabs_i.append(jnp.broadcast_to(ai.reshape(1, -1), (V7X_SUBLANES, ai.size)))
    return blockdiag_b(btr), blockdiag_b(bti), jnp.stack(tabs_r), jnp.stack(tabs_i)


def _s5_output_params(p):
    g, ch, st = p["s5_c_re"].shape
    nj = g // S5_LANE_BLOCK_GROUPS
    eye = jnp.eye(S5_LANE_BLOCK_GROUPS, dtype=F32)

    def blockdiag_c(c):
        c4 = c.astype(F32).reshape(nj, S5_LANE_BLOCK_GROUPS, ch, st)
        return jnp.einsum("jgcp,gh->jgphc", c4, eye).reshape(nj, S5_STATE_BLOCK, V7X_LANES).astype(BF16)

    return blockdiag_c(p["s5_c_re"]), blockdiag_c(p["s5_c_im"])


def _s5_kernel(*refs, reverse, finish, tseg, nj):
    if finish:
        (u_ref, h0r_ref, h0i_ref, bre_ref, bim_ref, cre_ref, cim_ref, tr_ref, ti_ref,
         yf_ref, d_ref, wglu_ref, y_ref, hfr_ref, hfi_ref,
         up_s, xr_s, xi_s, car_s, cai_s, yp_s, yt_s) = refs
    else:
        (u_ref, h0r_ref, h0i_ref, bre_ref, bim_ref, cre_ref, cim_ref, tr_ref, ti_ref,
         y_ref, hfr_ref, hfi_ref, up_s, xr_s, xi_s, car_s, cai_s) = refs

    @pl.when(pl.program_id(1) == 0)
    def _():
        car_s[...] = h0r_ref[0]
        cai_s[...] = h0i_ref[0]

    nseg = S5_SEGMENTS
    sb = S5_STATE_BLOCK
    tile = lambda t: pl.ds(pl.multiple_of(t * nseg, nseg), nseg)

    def gather(t, c):
        up_s[tile(t), :] = u_ref[0, :, t, :]
        return c

    lax.fori_loop(0, tseg, gather, 0)
    edge = 0 if reverse else nseg - 1
    first = nseg - 1 if reverse else 0
    for j in range(nj):
        lanes = slice(j * V7X_LANES, (j + 1) * V7X_LANES)
        states = slice(j * sb, (j + 1) * sb)
        ub = up_s[:, lanes].astype(BF16)
        xr_s[...] = jnp.dot(ub, bre_ref[j], preferred_element_type=F32)
        xi_s[...] = jnp.dot(ub, bim_ref[j], preferred_element_type=F32)
        ar = tr_ref[len(S5_LEVELS) + 1, :, states]
        ai = ti_ref[len(S5_LEVELS) + 1, :, states]

        def step(t, carry, store, ar=ar, ai=ai):
            hr, hi = carry
            rows = tile((tseg - 1 - t) if reverse else t)
            hr, hi = ar * hr - ai * hi + xr_s[rows, :], ar * hi + ai * hr + xi_s[rows, :]
            if store:
                xr_s[rows, :] = hr
                xi_s[rows, :] = hi
            return hr, hi

        zero = jnp.zeros((nseg, sb), F32)
        gr, gi = lax.fori_loop(0, tseg, functools.partial(step, store=False), (zero, zero), unroll=4)
        for lvl, s in enumerate(S5_LEVELS):
            pr = tr_ref[lvl, :, states]
            pi = ti_ref[lvl, :, states]
            shift = (nseg - s) if reverse else s
            sr = pltpu.roll(gr, shift, 0)
            si = pltpu.roll(gi, shift, 0)
            gr, gi = gr + pr * sr - pi * si, gi + pr * si + pi * sr
        pr = tr_ref[len(S5_LEVELS), :, states]
        pi = ti_ref[len(S5_LEVELS), :, states]
        cr = jnp.broadcast_to(car_s[:, states], (nseg, sb))
        ci = jnp.broadcast_to(cai_s[:, states], (nseg, sb))
        gr, gi = gr + pr * cr - pi * ci, gi + pr * ci + pi * cr
        car_s[:, states] = gr[edge:edge + 1, :]
        cai_s[:, states] = gi[edge:edge + 1, :]
        row = lax.broadcasted_iota(jnp.int32, (nseg, sb), 0)
        shift = (nseg - 1) if reverse else 1
        start = (jnp.where(row == first, cr, pltpu.roll(gr, shift, 0)),
                 jnp.where(row == first, ci, pltpu.roll(gi, shift, 0)))
        lax.fori_loop(0, tseg, functools.partial(step, store=True), start, unroll=4)
        yp = (jnp.dot(xr_s[...].astype(BF16), cre_ref[j], preferred_element_type=F32)
              - jnp.dot(xi_s[...].astype(BF16), cim_ref[j], preferred_element_type=F32))
        if finish:
            yp_s[:, lanes] = yp + yf_ref[0, :, lanes]
        else:
            y_ref[0, :, lanes] = yp

    hfr_ref[0] = car_s[...]
    hfi_ref[0] = cai_s[...]
    if finish:
        def scatter(t, c):
            yt_s[:, t, :] = yp_s[tile(t), :]
            return c

        lax.fori_loop(0, tseg, scatter, 0)
        for s in range(nseg):
            y = yt_s[s] + d_ref[...] * u_ref[0, s]
            gl = _gelu_tanh(y)
            gate = jnp.dot(gl.astype(BF16), wglu_ref[...], preferred_element_type=F32)
            y_ref[0, s * tseg:(s + 1) * tseg, :] = (gl * jax.nn.sigmoid(gate)).astype(y_ref.dtype)


def s5_scan(u, colblk, width, dirp, outp, h0, *, reverse, tseg, fin=None):
    b, l, ctot = u.shape
    nseg = S5_SEGMENTS
    tt = nseg * tseg
    nc = l // tt
    nj = width // V7X_LANES
    ns = nj * S5_STATE_BLOCK
    bre, bim, tr, ti = dirp
    cre, cim = outp
    h0r, h0i = h0
    cmap = (lambda bb, c: (bb, nc - 1 - c, 0, colblk)) if reverse else (lambda bb, c: (bb, c, 0, colblk))
    omap = (lambda bb, c: (bb, nc - 1 - c, 0)) if reverse else (lambda bb, c: (bb, c, 0))
    full = lambda shape: pl.BlockSpec(shape, lambda *_: (0,) * len(shape))
    st_spec = pl.BlockSpec((1, 1, ns), lambda bb, c: (bb, 0, 0))
    in_specs = [pl.BlockSpec((1, nseg, tseg, width), cmap), st_spec, st_spec,
                full(bre.shape), full(bim.shape), full(cre.shape), full(cim.shape),
                full(tr.shape), full(ti.shape)]
    args = [u.reshape(b, l // tseg, tseg, ctot), h0r, h0i, bre, bim, cre, cim, tr, ti]
    scratch = [pltpu.VMEM((tt, width), F32),
               pltpu.VMEM((tt, S5_STATE_BLOCK), F32), pltpu.VMEM((tt, S5_STATE_BLOCK), F32),
               pltpu.VMEM((1, ns), F32), pltpu.VMEM((1, ns), F32)]
    vmem = 3 * tt * width * 4 + 2 * tt * S5_STATE_BLOCK * 4 + 2 * tt * width * 4
    if fin is not None:
        y_other, d, wglu = fin
        in_specs += [pl.BlockSpec((1, tt, width), omap), full((1, width)), full(wglu.shape)]
        args += [y_other, d.reshape(1, width).astype(F32), wglu]
        scratch += [pltpu.VMEM((tt, width), F32), pltpu.VMEM((nseg, tseg, width), F32)]
        vmem += 4 * tt * width * 4 + 2 * wglu.size * 2
    out_dtype = BF16 if fin is not None else F32
    vmem += 2 * 4 * bre.size * 2 + 2 * 2 * tr.size * 4
    return pl.pallas_call(
        functools.partial(_s5_kernel, reverse=reverse, finish=fin is not None, tseg=tseg, nj=nj),
        grid=(b, nc),
        in_specs=in_specs,
        out_specs=[pl.BlockSpec((1, tt, width), omap), st_spec, st_spec],
        out_shape=[jax.ShapeDtypeStruct((b, l, width), out_dtype),
                   jax.ShapeDtypeStruct((b, 1, ns), F32), jax.ShapeDtypeStruct((b, 1, ns), F32)],
        scratch_shapes=scratch,
        compiler_params=_cparams(2, vmem),
        name="s5_scan_bwd" if reverse else "s5_scan_fwd",
    )(*args)


def s5_mix(u, colblk, width, p, h0f, h0b, tseg=128):
    tseg = min(tseg, u.shape[1] // S5_SEGMENTS)
    outp = _s5_output_params(p)
    yf, hf_r, hf_i = s5_scan(u, colblk, width, _s5_direction_params(p, 0, False, tseg), outp, h0f,
                             reverse=False, tseg=tseg)
    y, hb_r, hb_i = s5_scan(u, colblk, width, _s5_direction_params(p, 1, True, tseg), outp, h0b,
                            reverse=True, tseg=tseg, fin=(yf, p["s5_d"], p["s5_w_glu"].astype(BF16)))
    return y, (hf_r, hf_i), (hb_r, hb_i)


def _fill_padded(pad_s, main, prev, nxt, halo):
    i = pl.program_id(1)
    n = pl.num_programs(1)
    tt = main.shape[0]
    pad_s[0:halo, :] = jnp.where(i > 0, prev, 0.0)
    pad_s[halo:halo + tt, :] = main
    pad_s[halo + tt:halo + tt + halo, :] = jnp.where(i < n - 1, nxt, 0.0)


def _dwconv_rows(pad_s, w_ref, width, halo, tt, row_block=32):
    off = halo - (width - 1) // 2
    outs = []
    for r0 in range(0, tt, row_block):
        rb = min(row_block, tt - r0)
        acc = None
        for k in range(width):
            term = w_ref[k:k + 1, :] * pad_s[r0 + off + k:r0 + off + k + rb, :]
            acc = term if acc is None else acc + term
        outs.append(acc)
    return outs


def _halo_specs(tt, halo, width, colblk, l):
    r = tt // halo
    nh = l // halo
    main = pl.BlockSpec((1, tt, width), lambda bb, i: (bb, i, colblk))
    prev = pl.BlockSpec((1, halo, width), lambda bb, i: (bb, jnp.maximum(i * r - 1, 0), colblk))
    nxt = pl.BlockSpec((1, halo, width), lambda bb, i: (bb, jnp.minimum((i + 1) * r, nh - 1), colblk))
    return [main, prev, nxt]


CV_HALO = 16


CV_ROW_BLOCK = 64


def _conv_module_kernel(v_ref, vp_ref, vn_ref, g_ref, gp_ref, gn_ref, w_ref, b_ref, lg_ref, lb_ref,
                        o_ref, pad_s, acc_s, *, tt):
    glu = lambda v, g: v * jax.nn.sigmoid(g)
    _fill_padded(pad_s, glu(v_ref[0], g_ref[0]), glu(vp_ref[0], gp_ref[0]), glu(vn_ref[0], gn_ref[0]), CV_HALO)
    sub, rb = V7X_SUBLANES, CV_ROW_BLOCK
    off = CV_HALO - (CONV_WIDTH - 1) // 2

    def row_block(bi, c):
        r0 = pl.multiple_of(bi * rb, rb)
        for lb in range(acc_s.shape[1] // V7X_LANES):
            lanes = slice(lb * V7X_LANES, (lb + 1) * V7X_LANES)
            y = None
            for r in range(sub):
                part = None
                for m in range(off, off + CONV_WIDTH):
                    if m % sub != r:
                        continue
                    rows = pl.ds(pl.multiple_of(r0 + (m // sub) * sub, sub), rb + sub)
                    term = w_ref[m - off:m - off + 1, lanes] * pad_s[rows, lanes]
                    part = term if part is None else part + term
                if part is not None:
                    y = part[r:r + rb, :] if y is None else y + part[r:r + rb, :]
            acc_s[pl.ds(r0, rb), lanes] = y
        hcv = acc_s[pl.ds(r0, rb), :] + b_ref[...]
        mu = jnp.mean(hcv, axis=-1, keepdims=True)
        hc = hcv - mu
        var = jnp.mean(hc * hc, axis=-1, keepdims=True)
        yn = hc * lax.rsqrt(var + LN_EPS) * lg_ref[...] + lb_ref[...]
        o_ref[0, pl.ds(r0, rb), :] = _silu(yn).astype(o_ref.dtype)
        return c

    lax.fori_loop(0, tt // rb, row_block, 0)


def conv_module(u, colblk_v, colblk_g, width, p, tt=256):
    b, l, _ = u.shape
    tt = min(tt, l)
    assert tt % CV_ROW_BLOCK == 0
    wk = jnp.zeros((32, width), F32).at[:CONV_WIDTH].set(p["cv_w"].astype(F32))
    vec = pl.BlockSpec((1, width), lambda bb, i: (0, 0))
    return pl.pallas_call(
        functools.partial(_conv_module_kernel, tt=tt),
        grid=(b, l // tt),
        in_specs=_halo_specs(tt, CV_HALO, width, colblk_v, l) + _halo_specs(tt, CV_HALO, width, colblk_g, l)
        + [pl.BlockSpec((32, width), lambda bb, i: (0, 0)), vec, vec, vec],
        out_specs=pl.BlockSpec((1, tt, width), lambda bb, i: (bb, i, 0)),
        out_shape=jax.ShapeDtypeStruct((b, l, width), BF16),
        scratch_shapes=[pltpu.VMEM((tt + 2 * CV_HALO, width), F32), pltpu.VMEM((tt, width), F32)],
        compiler_params=_cparams(2, 12 * (tt + 2 * CV_HALO) * width * 4),
        name="conv_module",
    )(u, u, u, u, u, u, wk, p["cv_b"].reshape(1, width).astype(F32),
      p["cv_ln_g"].reshape(1, width).astype(F32), p["cv_ln_b"].reshape(1, width).astype(F32))


SSD_HALO = 8
SSD_HEADS_PER_GROUP = 4
SSD_PAIR = V7X_LANES // SSD_HEAD_DIM


def _ssd_conv_kernel(x_ref, xp_ref, xn_ref, w_ref, b_ref, o_ref, pad_s, *, tt):
    _fill_padded(pad_s, x_ref[0], xp_ref[0], xn_ref[0], SSD_HALO)
    row_block = 64
    for bi, acc in enumerate(_dwconv_rows(pad_s, w_ref, SSD_CONV, SSD_HALO, tt, row_block)):
        o_ref[0, bi * row_block:bi * row_block + acc.shape[0], :] = _silu(acc + b_ref[...])


def ssd_conv(u, colblk0, ncol, width, conv_w, conv_b, tt=256):
    b, l, _ = u.shape
    tt = min(tt, l)
    r = tt // SSD_HALO
    nh = l // SSD_HALO
    wk = jnp.zeros((V7X_SUBLANES, ncol * width), F32).at[:SSD_CONV].set(conv_w.astype(F32))
    return pl.pallas_call(
        functools.partial(_ssd_conv_kernel, tt=tt),
        grid=(b, l // tt, ncol),
        in_specs=[pl.BlockSpec((1, tt, width), lambda bb, i, c: (bb, i, colblk0 + c)),
                  pl.BlockSpec((1, SSD_HALO, width), lambda bb, i, c: (bb, jnp.maximum(i * r - 1, 0), colblk0 + c)),
                  pl.BlockSpec((1, SSD_HALO, width),
                               lambda bb, i, c: (bb, jnp.minimum((i + 1) * r, nh - 1), colblk0 + c)),
                  pl.BlockSpec((V7X_SUBLANES, width), lambda bb, i, c: (0, c)),
                  pl.BlockSpec((1, width), lambda bb, i, c: (0, c))],
        out_specs=pl.BlockSpec((1, tt, width), lambda bb, i, c: (bb, i, c)),
        out_shape=jax.ShapeDtypeStruct((b, l, ncol * width), F32),
        scratch_shapes=[pltpu.VMEM((tt + 2 * SSD_HALO, width), F32)],
        compiler_params=_cparams(3, 8 * (tt + 2 * SSD_HALO) * width * 4),
        name="ssd_conv",
    )(u, u, u, wk, conv_b.reshape(1, -1).astype(F32))


def _ssd_kernel(*refs, reverse, finish, q, gw, nheads):
    if finish:
        (xbc_ref, dtr_ref, dtb_ref, arow_ref, sel_ref, s0_ref, yf_ref, z_ref, dvec_ref, nw_ref,
         y_ref, sfin_ref, st_s, yb_s) = refs
    else:
        xbc_ref, dtr_ref, dtb_ref, arow_ref, sel_ref, s0_ref, y_ref, sfin_ref, st_s = refs
        yb_s = y_ref.at[0]

    @pl.when(pl.program_id(1) == 0)
    def _():
        st_s[...] = s0_ref[0]

    n = SSD_STATE
    dt = _softplus(dtr_ref[0] + dtb_ref[...])
    dta = dt * arow_ref[...]
    row = lax.broadcasted_iota(jnp.int32, (q, q), 0)
    col = lax.broadcasted_iota(jnp.int32, (q, q), 1)
    tri = (row <= col) if reverse else (row >= col)
    cum = jnp.dot(tri.astype(F32), dta, precision=HIGHEST, preferred_element_type=F32)
    cum_t = cum.T
    dt_t = dt.T
    last = cum[0:1, :] if reverse else cum[q - 1:q, :]
    last_t = cum_t[:, 0:1] if reverse else cum_t[:, q - 1:q]
    wdt_t = jnp.exp(last_t - cum_t) * dt_t
    elast = jnp.exp(last)
    ecum = jnp.exp(cum)
    e1 = ecum.astype(BF16)
    r1 = ecum - e1.astype(F32)
    e2 = r1.astype(BF16)
    e3 = (r1 - e2.astype(F32)).astype(BF16)
    ecum_wide = jnp.dot(jnp.concatenate([e1, e2, e3], axis=1), sel_ref[...], preferred_element_type=F32)
    lo_q = lax.broadcasted_iota(jnp.int32, (q, V7X_LANES), 1) < SSD_HEAD_DIM
    lo_n = lax.broadcasted_iota(jnp.int32, (n, V7X_LANES), 1) < SSD_HEAD_DIM
    b_off = nheads * SSD_HEAD_DIM
    c_off = b_off + SSD_NGROUPS * n
    for g in range(SSD_NGROUPS):
        bg = xbc_ref[0, :, b_off + g * n:b_off + (g + 1) * n]
        bg_t = bg.T
        cb = xbc_ref[0, :, c_off + g * n:c_off + (g + 1) * n].astype(BF16)
        scores = lax.dot_general(cb, bg.astype(BF16), (((1,), (1,)), ((), ())), preferred_element_type=F32)
        for pr in range(SSD_HEADS_PER_GROUP // SSD_PAIR):
            pidx = g * (SSD_HEADS_PER_GROUP // SSD_PAIR) + pr
            lanes = slice(pidx * V7X_LANES, (pidx + 1) * V7X_LANES)
            xp = xbc_ref[0, :, lanes].astype(BF16)
            yd, stn = [], []
            for hh in range(SSD_PAIR):
                h = pidx * SSD_PAIR + hh
                seg = cum[:, h:h + 1] - cum_t[h:h + 1, :]
                dec = jnp.where(tri, jnp.exp(jnp.where(tri, seg, 0.0)), 0.0)
                w = (scores * dec * dt_t[h:h + 1, :]).astype(BF16)
                yd.append(jnp.dot(w, xp, preferred_element_type=F32))
                wb = (bg_t * wdt_t[h:h + 1, :]).astype(BF16)
                stn.append(jnp.dot(wb, xp, preferred_element_type=F32))
            h0 = pidx * SSD_PAIR
            s_prev = st_s[pidx]
            y_off = jnp.dot(cb, s_prev.astype(BF16), preferred_element_type=F32) * ecum_wide[:, lanes]
            st_s[pidx] = (s_prev * jnp.where(lo_n, elast[:, h0:h0 + 1], elast[:, h0 + 1:h0 + 2])
                          + jnp.where(lo_n, stn[0], stn[1]))
            yb_s[:, lanes] = jnp.where(lo_q, yd[0], yd[1]) + y_off
    sfin_ref[0] = st_s[...]
    if finish:
        xs = xbc_ref[0, :, :b_off]
        y = yf_ref[0] + yb_s[...] + dvec_ref[...] * xs
        t = y * _silu(z_ref[0])
        for g in range(SSD_NGROUPS):
            tg = t[:, g * gw:(g + 1) * gw]
            ms = jnp.mean(tg * tg, axis=-1, keepdims=True)
            y_ref[0, :, g * gw:(g + 1) * gw] = (tg * lax.rsqrt(ms + RMS_EPS)
                                                * nw_ref[:, g * gw:(g + 1) * gw]).astype(y_ref.dtype)


def ssd_scan(xbc, dtraw, dcol, dt_bias, a_head, s0, *, reverse, fin=None, q=256):
    b, l, cw = xbc.shape
    q = min(q, l)
    nc = l // q
    nheads = dt_bias.shape[0]
    width = nheads * SSD_HEAD_DIM
    npair = nheads // SSD_PAIR
    pad = lambda v: jnp.zeros((1, V7X_LANES), F32).at[0, :nheads].set(v.astype(F32))
    cmap = (lambda bb, c: (bb, nc - 1 - c, 0)) if reverse else (lambda bb, c: (bb, c, 0))
    dmap = (lambda bb, c: (bb, nc - 1 - c, dcol)) if reverse else (lambda bb, c: (bb, c, dcol))
    full = lambda shape: pl.BlockSpec(shape, lambda *_: (0,) * len(shape))
    st_spec = pl.BlockSpec((1, npair, SSD_STATE, V7X_LANES), lambda bb, c: (bb, 0, 0, 0))
    head_of_lane = np.arange(width) // SSD_HEAD_DIM
    sel = (np.arange(V7X_LANES)[:, None] == head_of_lane[None, :]).astype(np.float32)
    sel = jnp.asarray(np.concatenate([sel, sel, sel], axis=0), BF16)
    in_specs = [pl.BlockSpec((1, q, cw), cmap), pl.BlockSpec((1, q, V7X_LANES), dmap),
                full((1, V7X_LANES)), full((1, V7X_LANES)), full(sel.shape), st_spec]
    args = [xbc, dtraw, pad(dt_bias), pad(a_head), sel, s0]
    scratch = [pltpu.VMEM((npair, SSD_STATE, V7X_LANES), F32)]
    if fin is not None:
        y_other, z, zcol, dvec, nw = fin
        zmap = (lambda bb, c: (bb, nc - 1 - c, zcol)) if reverse else (lambda bb, c: (bb, c, zcol))
        in_specs += [pl.BlockSpec((1, q, width), cmap), pl.BlockSpec((1, q, width), zmap),
                     full((1, width)), full((1, width))]
        args += [y_other, z, dvec, nw]
        scratch.append(pltpu.VMEM((q, width), F32))
    out_dtype = BF16 if fin is not None else F32
    return pl.pallas_call(
        functools.partial(_ssd_kernel, reverse=reverse, finish=fin is not None, q=q,
                          gw=width // SSD_NGROUPS, nheads=nheads),
        grid=(b, nc),
        in_specs=in_specs,
        out_specs=[pl.BlockSpec((1, q, width), cmap), st_spec],
        out_shape=[jax.ShapeDtypeStruct((b, l, width), out_dtype),
                   jax.ShapeDtypeStruct((b, npair, SSD_STATE, V7X_LANES), F32)],
        scratch_shapes=scratch,
        compiler_params=_cparams(2, 4 * q * cw * 4 + 12 * q * width * 4 + 24 * q * q * 4),
        name="ssd_scan_bwd" if reverse else "ssd_scan_fwd",
    )(*args)


def ssd_mix(u, dtraw, zcol, xcol0, width, p, s0f, s0b):
    nheads = p["ssd_dt_bias"].shape[1]
    xbc = ssd_conv(u, xcol0, 2, width, p["ssd_conv_w"], p["ssd_conv_b"])
    a_head = -jnp.exp(p["ssd_a_log"].astype(F32))
    yf, sf = ssd_scan(xbc, dtraw, 0, p["ssd_dt_bias"][0], a_head[0], s0f, reverse=False)
    dvec = jnp.repeat(p["ssd_d"].astype(F32), SSD_HEAD_DIM).reshape(1, width)
    y, sb = ssd_scan(xbc, dtraw, 1, p["ssd_dt_bias"][1], a_head[1], s0b, reverse=True,
                     fin=(yf, u, zcol, dvec, p["ssd_norm_w"].reshape(1, width).astype(F32)))
    return y, sf, sb


def _cast_pad_kernel(x_ref, o_ref, *, nvalid):
    j = pl.program_id(1)

    @pl.when(j < nvalid)
    def _():
        o_ref[...] = x_ref[...].astype(o_ref.dtype)

    @pl.when(j >= nvalid)
    def _():
        o_ref[...] = jnp.zeros_like(o_ref)


def cast_pad_bf16(w, axis, tile, total):
    depth, r, c = w.shape
    size = w.shape[axis]
    assert size % tile == 0 and total % tile == 0
    nvalid = size // tile
    if axis == 2:
        block, out_shape = (1, r, tile), (depth, r, total)
        imap = lambda l, j: (l, 0, jnp.minimum(j, nvalid - 1))
        omap = lambda l, j: (l, 0, j)
    else:
        block, out_shape = (1, tile, c), (depth, total, c)
        imap = lambda l, j: (l, jnp.minimum(j, nvalid - 1), 0)
        omap = lambda l, j: (l, j, 0)
    nelem = block[1] * block[2]
    return pl.pallas_call(
        functools.partial(_cast_pad_kernel, nvalid=nvalid),
        grid=(depth, total // tile),
        in_specs=[pl.BlockSpec(block, imap)],
        out_specs=pl.BlockSpec(block, omap),
        out_shape=jax.ShapeDtypeStruct(out_shape, BF16),
        compiler_params=_cparams(2, 2 * nelem * 6),
        name="cast_pad_bf16",
    )(w)


def _dt_weights(w_in, n_main, nheads):
    pad_last = lambda w: jnp.pad(w, ((0, 0), (0, 0), (0, V7X_LANES - w.shape[2])))
    return jnp.concatenate([pad_last(w_in[:, :, n_main:n_main + nheads]),
                            pad_last(w_in[:, :, n_main + nheads:])], axis=2).astype(BF16)


def _mixers(hm, wts, p, s5_h0, ssd_s0, want_output=True):
    b, l, d = hm.shape
    gw = d // N_MIXERS
    hm2 = hm.reshape(b * l, d)
    u = matmul(hm2, wts["w_in"], layer=wts["layer"], n_out=wts["n_main"], tn=1024,
               name="w_in_main").reshape(b, l, -1)
    dtraw = matmul(hm2, wts["w_dt"], tn=2 * V7X_LANES, name="w_in_dt").reshape(b, l, -1)
    ys, s5f, s5b = s5_mix(u, 1, gw, p, *s5_h0)
    yd, sdf, sdb = ssd_mix(u, dtraw, 4, 5, gw, p, *ssd_s0)
    states = ((s5f, s5b), (sdf, sdb))
    if not want_output:
        return None, states
    ya = fnet_mix(u, 0, gw)
    yc = conv_module(u, 2, 3, gw, p)
    return [y.reshape(b * l, gw) for y in (ya, ys, yc, yd)], states


def _layer_tail(h, mix, wts, mod, mod_next, ln, *, alpha, rows_per_batch):
    (ln1_g, ln1_b, ln2_g, ln2_b) = ln
    layer = wts["layer"]
    out = matmul(mix, wts["w_out"], layer=layer, tm=512, tn=1024, name="w_out")
    h, hm = residual_layernorm(h, out, mod[2], ln1_g, ln1_b, mod[4], mod[3], alpha=alpha,
                               rows_per_batch=rows_per_batch)
    hid = matmul(hm, [wts["w_gate"], wts["w_up"]], tm=1024, tn=512, out_dtype=BF16, name="gate_up")
    ff = matmul(hid, wts["w_down"], tm=512, tn=512, name="w_down")
    return residual_layernorm(h, ff, mod[5], ln2_g, ln2_b, mod_next[1], mod_next[0], alpha=alpha,
                              rows_per_batch=rows_per_batch)


def kernel(x, c, ctx, c_ctx, w_ada, b_ada, w_in, s5_lam_re, s5_lam_im, s5_log_dt, s5_b_re, s5_b_im,
           s5_c_re, s5_c_im, s5_d, s5_w_glu, cv_w, cv_b, cv_ln_g, cv_ln_b, ssd_conv_w, ssd_conv_b,
           ssd_a_log, ssd_dt_bias, ssd_d, ssd_norm_w, w_out, ln1_g, ln1_b, w_gate, w_up, w_down,
           ln2_g, ln2_b):
    bsz, seq, d = x.shape
    cl = ctx.shape[1]
    depth = w_in.shape[0]
    gw = d // N_MIXERS
    alpha = (2 * depth) ** 0.25
    per_layer = dict(s5_lam_re=s5_lam_re, s5_lam_im=s5_lam_im, s5_log_dt=s5_log_dt,
                     s5_b_re=s5_b_re, s5_b_im=s5_b_im, s5_c_re=s5_c_re, s5_c_im=s5_c_im, s5_d=s5_d,
                     s5_w_glu=s5_w_glu, cv_w=cv_w, cv_b=cv_b, cv_ln_g=cv_ln_g, cv_ln_b=cv_ln_b,
                     ssd_conv_w=ssd_conv_w, ssd_conv_b=ssd_conv_b, ssd_a_log=ssd_a_log,
                     ssd_dt_bias=ssd_dt_bias, ssd_d=ssd_d, ssd_norm_w=ssd_norm_w)

    c8 = jnp.zeros((V7X_SUBLANES, d), F32).at[:bsz].set(c).at[bsz].set(c_ctx)
    mods = ada_modulation(c8, w_ada, b_ada)

    def mod_vectors(i):
        m = mods[i].reshape(V7X_SUBLANES, 6, d)
        lat = [m[:bsz, k].reshape(bsz, 1, d) for k in range(6)]
        cx = [jnp.broadcast_to(m[bsz, k].reshape(1, 1, d), (bsz, 1, d)) for k in range(6)]
        return lat, cx

    nstate = (gw // V7X_LANES) * S5_STATE_BLOCK
    npair = ssd_dt_bias.shape[2] // SSD_PAIR
    z_s5 = (jnp.zeros((bsz, 1, nstate), F32), jnp.zeros((bsz, 1, nstate), F32))
    z_ssd = jnp.zeros((bsz, npair, SSD_STATE, V7X_LANES), F32)

    n_main = 4 * gw + gw + (gw + 2 * SSD_NGROUPS * SSD_STATE)
    w_dt = _dt_weights(w_in, n_main, ssd_dt_bias.shape[2])
    hidden = w_gate.shape[2]
    hidden_pad = -(-hidden // FFN_TILE) * FFN_TILE
    cast_tile = math.gcd(hidden, FFN_TILE)
    w_gate_bf = cast_pad_bf16(w_gate, 2, cast_tile, hidden_pad)
    w_up_bf = cast_pad_bf16(w_up, 2, cast_tile, hidden_pad)
    w_down_bf = cast_pad_bf16(w_down, 1, cast_tile, hidden_pad)
    lat0, cx0 = mod_vectors(0)
    h, hm = prologue(x, lat0[1], lat0[0])
    h = h.reshape(bsz * seq, d)
    hc = ctx.reshape(bsz * cl, d)
    hcm = modulate(ctx, cx0[1], cx0[0])
    for i in range(depth):
        with_ctx = i < depth - 1
        p = {k: v[i] for k, v in per_layer.items()}
        wts = dict(layer=i, n_main=n_main, w_in=w_in, w_dt=w_dt[i], w_out=w_out, w_gate=w_gate_bf[i],
                   w_up=w_up_bf[i], w_down=w_down_bf[i])
        lat, cx = mod_vectors(i)
        lat_next, cx_next = mod_vectors(min(i + 1, depth - 1))
        ln = (ln1_g[i], ln1_b[i], ln2_g[i], ln2_b[i])
        mix_c, ((c5f, c5b), (cdf, cdb)) = _mixers(hcm.reshape(bsz, cl, d), wts, p, (z_s5, z_s5),
                                                   (z_ssd, z_ssd), want_output=with_ctx)
        mix_l, _ = _mixers(hm.reshape(bsz, seq, d), wts, p, (c5f, c5b), (cdf, cdb))
        h, hm = _layer_tail(h, mix_l, wts, lat, lat_next, ln, alpha=alpha, rows_per_batch=seq)
        if with_ctx:
            hc, hcm = _layer_tail(hc, mix_c, wts, cx, cx_next, ln, alpha=alpha, rows_per_batch=cl)
    return h.reshape(bsz, seq, d).astype(x.dtype)
```

```python
import functools
import math

import jax
import jax.numpy as jnp
import numpy as np
from jax import lax
from jax.experimental import pallas as pl
from jax.experimental.pallas import tpu as pltpu

F32 = jnp.float32
BF16 = jnp.bfloat16

GRID_W = 64
N_MIXERS = 4
FNET_HEADS = 4
S5_GROUP_CH = 16
S5_STATE = 64
CONV_WIDTH = 31
SSD_HEAD_DIM = 64
SSD_NGROUPS = 4
SSD_STATE = 128
SSD_CONV = 5
LN_EPS = 1e-5
RMS_EPS = 1e-5

V7X_LANES = 128
V7X_SUBLANES = 8
V7X_VMEM_LIMIT_BYTES = 60000 * 1024
FFN_TILE = 512


def _cparams(n_axes, vmem_bytes):
    limit = int(min(max(vmem_bytes * 5 // 4 + (2 << 20), 16 << 20), V7X_VMEM_LIMIT_BYTES))
    return pltpu.CompilerParams(dimension_semantics=("arbitrary",) * n_axes,
                                vmem_limit_bytes=limit)


def _silu(x):
    return x * jax.nn.sigmoid(x)


def _gelu_tanh(x):
    return 0.5 * x * (1.0 + jnp.tanh(math.sqrt(2.0 / math.pi) * (x + 0.044715 * (x * x * x))))


def _softplus(x):
    return jnp.maximum(x, 0.0) + jnp.log1p(jnp.exp(-jnp.abs(x)))


def _ada_kernel(c_ref, w_ref, b_ref, o_ref):
    cs = _silu(c_ref[...])
    ch = cs.astype(BF16)
    cl = (cs - ch.astype(F32)).astype(BF16)
    w = w_ref[0]
    wh = w.astype(BF16)
    wl = (w - wh.astype(F32)).astype(BF16)
    rows = cs.shape[0]
    top = jnp.dot(jnp.concatenate([ch, cl], axis=0), wh, preferred_element_type=F32)
    o_ref[0] = top[:rows] + top[rows:] + jnp.dot(ch, wl, preferred_element_type=F32) + b_ref[0]


def ada_modulation(c8, w_ada, b_ada, tn=1024):
    depth, d, n = w_ada.shape
    tn = min(tn, n)
    return pl.pallas_call(
        _ada_kernel,
        grid=(depth, n // tn),
        in_specs=[pl.BlockSpec((V7X_SUBLANES, d), lambda l, j: (0, 0)),
                  pl.BlockSpec((1, d, tn), lambda l, j: (l, 0, j)),
                  pl.BlockSpec((1, 1, tn), lambda l, j: (l, 0, j))],
        out_specs=pl.BlockSpec((1, V7X_SUBLANES, tn), lambda l, j: (l, 0, j)),
        out_shape=jax.ShapeDtypeStruct((depth, V7X_SUBLANES, n), F32),
        compiler_params=_cparams(2, 2 * d * tn * 4 + 6 * d * tn * 2),
        name="ada_modulation",
    )(c8, w_ada, b_ada.reshape(depth, 1, n))


def _prologue_kernel(x_ref, tr_ref, tc_ref, sc_ref, sh_ref, h_ref, hm_ref, *, tp, half):
    sc = 1.0 + sc_ref[0]
    sh = sh_ref[0]
    for rr in range(tp // GRID_W):
        rows = slice(rr * GRID_W, (rr + 1) * GRID_W)
        lo = x_ref[0, rows, :half] + tr_ref[rr:rr + 1, :]
        hi = x_ref[0, rows, half:] + tc_ref[...]
        h_ref[0, rows, :half] = lo
        h_ref[0, rows, half:] = hi
        hm_ref[0, rows, :half] = (lo * sc[:, :half] + sh[:, :half]).astype(BF16)
        hm_ref[0, rows, half:] = (hi * sc[:, half:] + sh[:, half:]).astype(BF16)


def prologue(x, sc, sh, tp=512):
    b, l, d = x.shape
    tp = min(tp, l)
    half = d // 2
    q = d // 4
    omega = jnp.exp(-math.log(10000.0) * jnp.arange(q, dtype=F32) / q)[None]
    r = jnp.arange(l // GRID_W, dtype=F32)[:, None]
    col = jnp.arange(GRID_W, dtype=F32)[:, None]
    tab_r = jnp.concatenate([jnp.sin(r * omega), jnp.cos(r * omega)], -1)
    tab_c = jnp.concatenate([jnp.sin(col * omega), jnp.cos(col * omega)], -1)
    rpt = tp // GRID_W
    return pl.pallas_call(
        functools.partial(_prologue_kernel, tp=tp, half=half),
        grid=(b, l // tp),
        in_specs=[pl.BlockSpec((1, tp, d), lambda bb, i: (bb, i, 0)),
                  pl.BlockSpec((rpt, half), lambda bb, i: (i, 0)),
                  pl.BlockSpec((GRID_W, half), lambda bb, i: (0, 0)),
                  pl.BlockSpec((1, 1, d), lambda bb, i: (bb, 0, 0)),
                  pl.BlockSpec((1, 1, d), lambda bb, i: (bb, 0, 0))],
        out_specs=[pl.BlockSpec((1, tp, d), lambda bb, i: (bb, i, 0)),
                   pl.BlockSpec((1, tp, d), lambda bb, i: (bb, i, 0))],
        out_shape=[jax.ShapeDtypeStruct((b, l, d), F32), jax.ShapeDtypeStruct((b, l, d), BF16)],
        compiler_params=_cparams(2, 2 * tp * d * (4 + 4 + 2)),
        name="prologue",
    )(x, tab_r, tab_c, sc, sh)


def _modulate_kernel(x_ref, sc_ref, sh_ref, o_ref):
    o_ref[0] = (x_ref[0] * (1.0 + sc_ref[0]) + sh_ref[0]).astype(BF16)


def modulate(x, sc, sh, tp=256):
    b, l, d = x.shape
    tp = min(tp, l)
    return pl.pallas_call(
        _modulate_kernel,
        grid=(b, l // tp),
        in_specs=[pl.BlockSpec((1, tp, d), lambda bb, i: (bb, i, 0)),
                  pl.BlockSpec((1, 1, d), lambda bb, i: (bb, 0, 0)),
                  pl.BlockSpec((1, 1, d), lambda bb, i: (bb, 0, 0))],
        out_specs=pl.BlockSpec((1, tp, d), lambda bb, i: (bb, i, 0)),
        out_shape=jax.ShapeDtypeStruct((b, l, d), BF16),
        compiler_params=_cparams(2, 2 * tp * d * 6),
        name="modulate",
    )(x, sc, sh)


def _mm_kernel(*refs, n_lhs, n_w, cast_w):
    x_refs = refs[:n_lhs]
    w_refs = refs[n_lhs:n_lhs + n_w]
    o_ref = refs[n_lhs + n_w]
    if cast_w:
        wb = refs[n_lhs + n_w + 1:]

        @pl.when(pl.program_id(1) == 0)
        def _():
            for w_ref, s in zip(w_refs, wb):
                s[...] = w_ref[...].astype(BF16)

        ws = [s[...] for s in wb]
    else:
        ws = [w_ref[...] for w_ref in w_refs]
    xs = [x_ref[...].astype(BF16) for x_ref in x_refs]
    x = xs[0] if n_lhs == 1 else jnp.concatenate(xs, axis=1)
    acc = [jnp.dot(x, w, preferred_element_type=F32) for w in ws]
    y = acc[0] if n_w == 1 else _silu(acc[0]) * acc[1]
    o_ref[...] = y.astype(o_ref.dtype)


def matmul(xs, ws, *, layer, n_out=None, tm=1024, tn=512, out_dtype=F32, name="matmul"):
    slabs = list(xs) if isinstance(xs, (list, tuple)) else [xs]
    wl = list(ws) if isinstance(ws, (list, tuple)) else [ws]
    m = slabs[0].shape[0]
    kk, n = wl[0].shape[-2:]
    n = n if n_out is None else n_out
    tm, tn = min(tm, m), min(tn, n)
    assert m % tm == 0 and n % tn == 0
    cast_w = wl[0].dtype != BF16
    x_specs = [pl.BlockSpec((tm, s.shape[1]), lambda j, i: (i, 0)) for s in slabs]
    if cast_w:
        w_specs = [pl.BlockSpec((None, kk, tn), lambda j, i: (layer, 0, j), pipeline_mode=pl.Buffered(1))
                   for _ in wl]
        scratch = [pltpu.VMEM((kk, tn), BF16) for _ in wl]
    else:
        w_specs = [pl.BlockSpec((None, kk, tn), lambda j, i: (layer, 0, j)) for _ in wl]
        scratch = []
    ob = jnp.dtype(out_dtype).itemsize
    xbytes = sum(tm * s.shape[1] * s.dtype.itemsize for s in slabs)
    wbytes = len(wl) * kk * tn * (4 + 2 if cast_w else 2 * 2)
    return pl.pallas_call(
        functools.partial(_mm_kernel, n_lhs=len(slabs), n_w=len(wl), cast_w=cast_w),
        grid=(n // tn, m // tm),
        in_specs=x_specs + w_specs,
        out_specs=pl.BlockSpec((tm, tn), lambda j, i: (i, j)),
        out_shape=jax.ShapeDtypeStruct((m, n), out_dtype),
        scratch_shapes=scratch,
        compiler_params=_cparams(2, 2 * xbytes + wbytes + 2 * tm * tn * ob + (len(wl) + 1) * tm * tn * 4
                                 + (tm * kk * 2 if len(slabs) > 1 else 0)),
        name=name,
    )(*slabs, *wl)


def _ln_kernel(h_ref, mix_ref, g_ref, lng_ref, lnb_ref, sc_ref, sh_ref, ho_ref, hm_ref, *, alpha):
    r = alpha * h_ref[...] + g_ref[0] * mix_ref[...]
    mu = jnp.mean(r, axis=-1, keepdims=True)
    rc = r - mu
    var = jnp.mean(rc * rc, axis=-1, keepdims=True)
    y = rc * lax.rsqrt(var + LN_EPS) * lng_ref[...] + lnb_ref[...]
    ho_ref[...] = y
    hm_ref[...] = (y * (1.0 + sc_ref[0]) + sh_ref[0]).astype(BF16)


def residual_layernorm(h, mix, gate, ln_g, ln_b, sc, sh, *, alpha, rows_per_batch, tm=256):
    m, n = h.shape
    tm = min(tm, m, rows_per_batch)
    row = pl.BlockSpec((tm, n), lambda i: (i, 0))
    vec_b = pl.BlockSpec((1, 1, n), lambda i: ((i * tm) // rows_per_batch, 0, 0))
    vec = pl.BlockSpec((1, n), lambda i: (0, 0))
    return pl.pallas_call(
        functools.partial(_ln_kernel, alpha=alpha),
        grid=(m // tm,),
        in_specs=[row, row, vec_b, vec, vec, vec_b, vec_b],
        out_specs=[row, row],
        out_shape=[jax.ShapeDtypeStruct((m, n), F32), jax.ShapeDtypeStruct((m, n), BF16)],
        compiler_params=_cparams(1, 2 * tm * n * (4 + 4 + 4 + 2) + 4 * tm * n * 4),
        name="residual_layernorm",
    )(h, mix, gate, ln_g.reshape(1, n), ln_b.reshape(1, n), sc, sh)


FNET_N2 = 128


def _dft_cos_sin(n):
    k = np.arange(n, dtype=np.int64)
    ang = (2.0 * np.pi / n) * ((k[:, None] * k[None, :]) % n).astype(np.float64)
    return np.cos(ang), np.sin(ang)


def _split_bf16(m):
    m = jnp.asarray(m, F32)
    hi = m.astype(BF16)
    return hi, (m - hi.astype(F32)).astype(BF16)


def _const_lhs3(m):
    hi, lo = _split_bf16(m)
    return jnp.concatenate([hi, hi, lo], axis=1)


def _const_rhs3(m):
    hi, lo = _split_bf16(m)
    return jnp.concatenate([hi, hi, lo], axis=0)


def _dot3_const_lhs(m3, x):
    xh = x.astype(BF16)
    xl = (x - xh.astype(F32)).astype(BF16)
    return jnp.dot(m3, jnp.concatenate([xh, xl, xh], axis=0), preferred_element_type=F32)


def _dot3_const_rhs(x, r3):
    xh = x.astype(BF16)
    xl = (x - xh.astype(F32)).astype(BF16)
    return jnp.dot(jnp.concatenate([xh, xl, xh], axis=1), r3, preferred_element_type=F32)


def _channel_dft(x, cs3_ref, hd):
    ps, qs = [], []
    for hh in range(x.shape[1] // hd):
        pq = _dot3_const_rhs(x[:, hh * hd:(hh + 1) * hd], cs3_ref[...])
        ps.append(pq[:, :hd])
        qs.append(pq[:, hd:])
    return jnp.concatenate(ps, axis=1), jnp.concatenate(qs, axis=1)


def _fnet_a_kernel(x_ref, cs3_ref, m1_ref, twr_ref, twi_ref, ar_ref, ai_ref, *, hd, n1):
    nsub = x_ref.shape[2]
    x = jnp.concatenate([x_ref[0, :, j, :] for j in range(nsub)], axis=0)
    p, q = _channel_dft(x, cs3_ref, hd)
    reps = x.shape[1] // V7X_LANES
    for j in range(nsub):
        rows = slice(j * n1, (j + 1) * n1)
        mp = _dot3_const_lhs(m1_ref[...], p[rows])
        mq = _dot3_const_lhs(m1_ref[...], q[rows])
        ar = mp[:n1] - mq[n1:]
        ai = -(mq[:n1] + mp[n1:])
        tr = jnp.tile(twr_ref[j], (1, reps))
        ti = jnp.tile(twi_ref[j], (1, reps))
        ar_ref[0, j] = ar * tr - ai * ti
        ai_ref[0, j] = ar * ti + ai * tr


def _fnet_c_kernel(ar_ref, ai_ref, m2_ref, o_ref, *, scale):
    for j in range(ar_ref.shape[2]):
        z = jnp.concatenate([ar_ref[0, :, j, :], ai_ref[0, :, j, :]], axis=0)
        o_ref[0, :, j, :] = _dot3_const_lhs(m2_ref[...], z) * scale


def _fnet_dense_kernel(x_ref, cs3_ref, ml_ref, o_ref, *, hd, scale):
    p, q = _channel_dft(x_ref[0], cs3_ref, hd)
    o_ref[0] = _dot3_const_lhs(ml_ref[...], jnp.concatenate([p, q], axis=0)) * scale


def fnet_mix(u, colblk, width):
    b, l, ctot = u.shape
    hd = width // FNET_HEADS
    scale = 1.0 / math.sqrt(l * hd)
    cd, sd = _dft_cos_sin(hd)
    cs3 = _const_rhs3(np.concatenate([cd, sd], axis=1))
    full = lambda shape: pl.BlockSpec(shape, lambda *_: (0,) * len(shape))
    n2 = FNET_N2
    if l < V7X_SUBLANES * n2:
        cl, sl = _dft_cos_sin(l)
        ml = _const_lhs3(np.concatenate([cl, -sl], axis=1))
        return pl.pallas_call(
            functools.partial(_fnet_dense_kernel, hd=hd, scale=scale),
            grid=(b,),
            in_specs=[pl.BlockSpec((1, l, width), lambda bb: (bb, 0, colblk)), full(cs3.shape), full(ml.shape)],
            out_specs=pl.BlockSpec((1, l, width), lambda bb: (bb, 0, 0)),
            out_shape=jax.ShapeDtypeStruct((b, l, width), F32),
            compiler_params=_cparams(1, 16 * l * width * 4),
            name="fnet_dense",
        )(u, cs3, ml)
    n1 = l // n2
    nsub = V7X_SUBLANES
    c1, s1 = _dft_cos_sin(n1)
    c2, s2 = _dft_cos_sin(n2)
    m1 = _const_lhs3(np.concatenate([c1, s1], axis=0))
    m2 = _const_lhs3(np.concatenate([c2, s2], axis=1))
    kk = (np.arange(n2, dtype=np.int64)[:, None] * np.arange(n1, dtype=np.int64)[None, :]) % l
    ang = (2.0 * np.pi / l) * kk.astype(np.float64)
    twr = jnp.asarray(np.broadcast_to(np.cos(ang)[:, :, None], (n2, n1, V7X_LANES)), F32)
    twi = jnp.asarray(np.broadcast_to(-np.sin(ang)[:, :, None], (n2, n1, V7X_LANES)), F32)
    ar, ai = pl.pallas_call(
        functools.partial(_fnet_a_kernel, hd=hd, n1=n1),
        grid=(b, n2 // nsub),
        in_specs=[pl.BlockSpec((1, n1, nsub, width), lambda bb, j: (bb, 0, j, colblk)),
                  full(cs3.shape), full(m1.shape),
                  pl.BlockSpec((nsub, n1, V7X_LANES), lambda bb, j: (j, 0, 0)),
                  pl.BlockSpec((nsub, n1, V7X_LANES), lambda bb, j: (j, 0, 0))],
        out_specs=[pl.BlockSpec((1, nsub, n1, width), lambda bb, j: (bb, j, 0, 0))] * 2,
        out_shape=[jax.ShapeDtypeStruct((b, n2, n1, width), F32)] * 2,
        compiler_params=_cparams(2, 40 * nsub * n1 * width * 4),
        name="fnet_stage_a",
    )(u.reshape(b, n1, n2, ctot), cs3, m1, twr, twi)
    y = pl.pallas_call(
        functools.partial(_fnet_c_kernel, scale=scale),
        grid=(b, n1 // nsub),
        in_specs=[pl.BlockSpec((1, n2, nsub, width), lambda bb, j: (bb, 0, j, 0)),
                  pl.BlockSpec((1, n2, nsub, width), lambda bb, j: (bb, 0, j, 0)),
                  full(m2.shape)],
        out_specs=pl.BlockSpec((1, n2, nsub, width), lambda bb, j: (bb, 0, j, 0)),
        out_shape=jax.ShapeDtypeStruct((b, n2, n1, width), F32),
        compiler_params=_cparams(2, 10 * n2 * nsub * width * 4),
        name="fnet_stage_c",
    )(ar, ai, m2)
    return y.reshape(b, l, width)


S5_LANE_BLOCK_GROUPS = V7X_LANES // S5_GROUP_CH
S5_STATE_BLOCK = S5_LANE_BLOCK_GROUPS * S5_STATE
S5_LEVELS = (1, 2, 4)
S5_SEGMENTS = V7X_SUBLANES


def _s5_direction_params(p, k, reverse, tseg):
    g, st = p["s5_lam_re"].shape[1:]
    lr, li = p["s5_lam_re"][k].astype(F32), p["s5_lam_im"][k].astype(F32)
    dt = jnp.exp(p["s5_log_dt"][k].astype(F32))[:, None]
    zr, zi = lr * dt, li * dt

    def power(s):
        e = jnp.exp(s * zr)
        return e * jnp.cos(s * zi), e * jnp.sin(s * zi)

    ar, ai = power(1.0)
    nr, ni = ar - 1.0, ai
    den = lr * lr + li * li
    cr, ci = (nr * lr + ni * li) / den, (ni * lr - nr * li) / den
    b_re, b_im = p["s5_b_re"].astype(F32), p["s5_b_im"].astype(F32)
    btr = b_re * cr[..., None] - b_im * ci[..., None]
    bti = b_re * ci[..., None] + b_im * cr[..., None]
    nj = g // S5_LANE_BLOCK_GROUPS
    eye = jnp.eye(S5_LANE_BLOCK_GROUPS, dtype=F32)

    def blockdiag_b(bt):
        b4 = bt.reshape(nj, S5_LANE_BLOCK_GROUPS, st, S5_GROUP_CH)
        return jnp.einsum("jgpc,gh->jgchp", b4, eye).reshape(nj, V7X_LANES, S5_STATE_BLOCK).astype(BF16)

    t = jnp.arange(V7X_SUBLANES)
    tabs_r, tabs_i = [], []
    for s in S5_LEVELS:
        pr, pi = power(float(s * tseg))
        keep = (t <= V7X_SUBLANES - 1 - s) if reverse else (t >= s)
        tabs_r.append(jnp.where(keep[:, None], pr.reshape(1, -1), 0.0))
        tabs_i.append(jnp.where(keep[:, None], pi.reshape(1, -1), 0.0))
    steps = ((V7X_SUBLANES - t) if reverse else (t + 1))[:, None, None].astype(F32) * float(tseg)
    e = jnp.exp(steps * zr[None])
    tabs_r.append((e * jnp.cos(steps * zi[None])).reshape(V7X_SUBLANES, -1))
    tabs_i.append((e * jnp.sin(steps * zi[None])).reshape(V7X_SUBLANES, -1))
    tabs_r.append(jnp.broadcast_to(ar.reshape(1, -1), (V7X_SUBLANES, ar.size)))
    tabs_i.append(jnp.broadcast_to(ai.reshape(1, -1), (V7X_SUBLANES, ai.size)))
    return blockdiag_b(btr), blockdiag_b(bti), jnp.stack(tabs_r), jnp.stack(tabs_i)


def _s5_output_params(p):
    g, ch, st = p["s5_c_re"].shape
    nj = g // S5_LANE_BLOCK_GROUPS
    eye = jnp.eye(S5_LANE_BLOCK_GROUPS, dtype=F32)

    def blockdiag_c(c):
        c4 = c.astype(F32).reshape(nj, S5_LANE_BLOCK_GROUPS, ch, st)
        return jnp.einsum("jgcp,gh->jgphc", c4, eye).reshape(nj, S5_STATE_BLOCK, V7X_LANES).astype(BF16)

    return blockdiag_c(p["s5_c_re"]), blockdiag_c(p["s5_c_im"])


def _s5_kernel(*refs, reverse, finish, tseg, nj):
    if finish:
        (u_ref, h0r_ref, h0i_ref, bre_ref, bim_ref, cre_ref, cim_ref, tr_ref, ti_ref,
         yf_ref, d_ref, wglu_ref, y_ref, hfr_ref, hfi_ref,
         up_s, xr_s, xi_s, car_s, cai_s, yp_s, yt_s) = refs
    else:
        (u_ref, h0r_ref, h0i_ref, bre_ref, bim_ref, cre_ref, cim_ref, tr_ref, ti_ref,
         y_ref, hfr_ref, hfi_ref, up_s, xr_s, xi_s, car_s, cai_s) = refs

    @pl.when(pl.program_id(1) == 0)
    def _():
        car_s[...] = h0r_ref[0]
        cai_s[...] = h0i_ref[0]

    nseg = S5_SEGMENTS
    sb = S5_STATE_BLOCK
    tile = lambda t: pl.ds(pl.multiple_of(t * nseg, nseg), nseg)

    def gather(t, c):
        up_s[tile(t), :] = u_ref[0, :, t, :]
        return c

    lax.fori_loop(0, tseg, gather, 0)
    edge = 0 if reverse else nseg - 1
    first = nseg - 1 if reverse else 0
    for j in range(nj):
        lanes = slice(j * V7X_LANES, (j + 1) * V7X_LANES)
        states = slice(j * sb, (j + 1) * sb)
        ub = up_s[:, lanes].astype(BF16)
        xr_s[...] = jnp.dot(ub, bre_ref[j], preferred_element_type=F32)
        xi_s[...] = jnp.dot(ub, bim_ref[j], preferred_element_type=F32)
        ar = tr_ref[len(S5_LEVELS) + 1, :, states]
        ai = ti_ref[len(S5_LEVELS) + 1, :, states]

        def step(t, carry, store, ar=ar, ai=ai):
            hr, hi = carry
            rows = tile((tseg - 1 - t) if reverse else t)
            hr, hi = ar * hr - ai * hi + xr_s[rows, :], ar * hi + ai * hr + xi_s[rows, :]
            if store:
                xr_s[rows, :] = hr
                xi_s[rows, :] = hi
            return hr, hi

        zero = jnp.zeros((nseg, sb), F32)
        gr, gi = lax.fori_loop(0, tseg, functools.partial(step, store=False), (zero, zero), unroll=4)
        for lvl, s in enumerate(S5_LEVELS):
            pr = tr_ref[lvl, :, states]
            pi = ti_ref[lvl, :, states]
            shift = (nseg - s) if reverse else s
            sr = pltpu.roll(gr, shift, 0)
            si = pltpu.roll(gi, shift, 0)
            gr, gi = gr + pr * sr - pi * si, gi + pr * si + pi * sr
        pr = tr_ref[len(S5_LEVELS), :, states]
        pi = ti_ref[len(S5_LEVELS), :, states]
        cr = jnp.broadcast_to(car_s[:, states], (nseg, sb))
        ci = jnp.broadcast_to(cai_s[:, states], (nseg, sb))
        gr, gi = gr + pr * cr - pi * ci, gi + pr * ci + pi * cr
        car_s[:, states] = gr[edge:edge + 1, :]
        cai_s[:, states] = gi[edge:edge + 1, :]
        row = lax.broadcasted_iota(jnp.int32, (nseg, sb), 0)
        shift = (nseg - 1) if reverse else 1
        start = (jnp.where(row == first, cr, pltpu.roll(gr, shift, 0)),
                 jnp.where(row == first, ci, pltpu.roll(gi, shift, 0)))
        lax.fori_loop(0, tseg, functools.partial(step, store=True), start, unroll=4)
        yp = (jnp.dot(xr_s[...].astype(BF16), cre_ref[j], preferred_element_type=F32)
              - jnp.dot(xi_s[...].astype(BF16), cim_ref[j], preferred_element_type=F32))
        if finish:
            yp_s[:, lanes] = yp + yf_ref[0, :, lanes]
        else:
            y_ref[0, :, lanes] = yp

    hfr_ref[0] = car_s[...]
    hfi_ref[0] = cai_s[...]
    if finish:
        def scatter(t, c):
            yt_s[:, t, :] = yp_s[tile(t), :]
            return c

        lax.fori_loop(0, tseg, scatter, 0)
        for s in range(nseg):
            y = yt_s[s] + d_ref[...] * u_ref[0, s]
            gl = _gelu_tanh(y)
            gate = jnp.dot(gl.astype(BF16), wglu_ref[...], preferred_element_type=F32)
            y_ref[0, s * tseg:(s + 1) * tseg, :] = (gl * jax.nn.sigmoid(gate)).astype(y_ref.dtype)


def s5_scan(u, colblk, width, dirp, outp, h0, *, reverse, tseg, fin=None):
    b, l, ctot = u.shape
    nseg = S5_SEGMENTS
    tt = nseg * tseg
    nc = l // tt
    nj = width // V7X_LANES
    ns = nj * S5_STATE_BLOCK
    bre, bim, tr, ti = dirp
    cre, cim = outp
    h0r, h0i = h0
    cmap = (lambda bb, c: (bb, nc - 1 - c, 0, colblk)) if reverse else (lambda bb, c: (bb, c, 0, colblk))
    omap = (lambda bb, c: (bb, nc - 1 - c, 0)) if reverse else (lambda bb, c: (bb, c, 0))
    full = lambda shape: pl.BlockSpec(shape, lambda *_: (0,) * len(shape))
    st_spec = pl.BlockSpec((1, 1, ns), lambda bb, c: (bb, 0, 0))
    in_specs = [pl.BlockSpec((1, nseg, tseg, width), cmap), st_spec, st_spec,
                full(bre.shape), full(bim.shape), full(cre.shape), full(cim.shape),
                full(tr.shape), full(ti.shape)]
    args = [u.reshape(b, l // tseg, tseg, ctot), h0r, h0i, bre, bim, cre, cim, tr, ti]
    scratch = [pltpu.VMEM((tt, width), F32),
               pltpu.VMEM((tt, S5_STATE_BLOCK), F32), pltpu.VMEM((tt, S5_STATE_BLOCK), F32),
               pltpu.VMEM((1, ns), F32), pltpu.VMEM((1, ns), F32)]
    vmem = 3 * tt * width * 4 + 2 * tt * S5_STATE_BLOCK * 4 + 2 * tt * width * 4
    if fin is not None:
        y_other, d, wglu = fin
        in_specs += [pl.BlockSpec((1, tt, width), omap), full((1, width)), full(wglu.shape)]
        args += [y_other, d.reshape(1, width).astype(F32), wglu]
        scratch += [pltpu.VMEM((tt, width), F32), pltpu.VMEM((nseg, tseg, width), F32)]
        vmem += 4 * tt * width * 4 + 2 * wglu.size * 2
    out_dtype = BF16 if fin is not None else F32
    vmem += 2 * 4 * bre.size * 2 + 2 * 2 * tr.size * 4
    return pl.pallas_call(
        functools.partial(_s5_kernel, reverse=reverse, finish=fin is not None, tseg=tseg, nj=nj),
        grid=(b, nc),
        in_specs=in_specs,
        out_specs=[pl.BlockSpec((1, tt, width), omap), st_spec, st_spec],
        out_shape=[jax.ShapeDtypeStruct((b, l, width), out_dtype),
                   jax.ShapeDtypeStruct((b, 1, ns), F32), jax.ShapeDtypeStruct((b, 1, ns), F32)],
        scratch_shapes=scratch,
        compiler_params=_cparams(2, vmem),
        name="s5_scan_bwd" if reverse else "s5_scan_fwd",
    )(*args)


def s5_mix(u, colblk, width, p, h0f, h0b, tseg=128):
    tseg = min(tseg, u.shape[1] // S5_SEGMENTS)
    outp = _s5_output_params(p)
    yf, hf_r, hf_i = s5_scan(u, colblk, width, _s5_direction_params(p, 0, False, tseg), outp, h0f,
                             reverse=False, tseg=tseg)
    y, hb_r, hb_i = s5_scan(u, colblk, width, _s5_direction_params(p, 1, True, tseg), outp, h0b,
                            reverse=True, tseg=tseg, fin=(yf, p["s5_d"], p["s5_w_glu"].astype(BF16)))
    return y, (hf_r, hf_i), (hb_r, hb_i)


def _fill_padded(pad_s, main, prev, nxt, halo):
    i = pl.program_id(1)
    n = pl.num_programs(1)
    tt = main.shape[0]
    pad_s[0:halo, :] = jnp.where(i > 0, prev, 0.0)
    pad_s[halo:halo + tt, :] = main
    pad_s[halo + tt:halo + tt + halo, :] = jnp.where(i < n - 1, nxt, 0.0)


def _dwconv_rows(pad_s, w_ref, width, halo, tt, row_block=32):
    off = halo - (width - 1) // 2
    outs = []
    for r0 in range(0, tt, row_block):
        rb = min(row_block, tt - r0)
        acc = None
        for k in range(width):
            term = w_ref[k:k + 1, :] * pad_s[r0 + off + k:r0 + off + k + rb, :]
            acc = term if acc is None else acc + term
        outs.append(acc)
    return outs


def _halo_specs(tt, halo, width, colblk, l):
    r = tt // halo
    nh = l // halo
    main = pl.BlockSpec((1, tt, width), lambda bb, i: (bb, i, colblk))
    prev = pl.BlockSpec((1, halo, width), lambda bb, i: (bb, jnp.maximum(i * r - 1, 0), colblk))
    nxt = pl.BlockSpec((1, halo, width), lambda bb, i: (bb, jnp.minimum((i + 1) * r, nh - 1), colblk))
    return [main, prev, nxt]


CV_HALO = 16


CV_ROW_BLOCK = 64


def _conv_module_kernel(v_ref, vp_ref, vn_ref, g_ref, gp_ref, gn_ref, w_ref, b_ref, lg_ref, lb_ref,
                        o_ref, pad_s, acc_s, *, tt):
    glu = lambda v, g: v * jax.nn.sigmoid(g)
    _fill_padded(pad_s, glu(v_ref[0], g_ref[0]), glu(vp_ref[0], gp_ref[0]), glu(vn_ref[0], gn_ref[0]), CV_HALO)
    sub, rb = V7X_SUBLANES, CV_ROW_BLOCK
    off = CV_HALO - (CONV_WIDTH - 1) // 2

    def row_block(bi, c):
        r0 = pl.multiple_of(bi * rb, rb)
        for lb in range(acc_s.shape[1] // V7X_LANES):
            lanes = slice(lb * V7X_LANES, (lb + 1) * V7X_LANES)
            y = None
            for r in range(sub):
                part = None
                for m in range(off, off + CONV_WIDTH):
                    if m % sub != r:
                        continue
                    rows = pl.ds(pl.multiple_of(r0 + (m // sub) * sub, sub), rb + sub)
                    term = w_ref[m - off:m - off + 1, lanes] * pad_s[rows, lanes]
                    part = term if part is None else part + term
                if part is not None:
                    y = part[r:r + rb, :] if y is None else y + part[r:r + rb, :]
            acc_s[pl.ds(r0, rb), lanes] = y
        hcv = acc_s[pl.ds(r0, rb), :] + b_ref[...]
        mu = jnp.mean(hcv, axis=-1, keepdims=True)
        hc = hcv - mu
        var = jnp.mean(hc * hc, axis=-1, keepdims=True)
        yn = hc * lax.rsqrt(var + LN_EPS) * lg_ref[...] + lb_ref[...]
        o_ref[0, pl.ds(r0, rb), :] = _silu(yn).astype(o_ref.dtype)
        return c

    lax.fori_loop(0, tt // rb, row_block, 0)


def conv_module(u, colblk_v, colblk_g, width, p, tt=256):
    b, l, _ = u.shape
    tt = min(tt, l)
    assert tt % CV_ROW_BLOCK == 0
    wk = jnp.zeros((32, width), F32).at[:CONV_WIDTH].set(p["cv_w"].astype(F32))
    vec = pl.BlockSpec((1, width), lambda bb, i: (0, 0))
    return pl.pallas_call(
        functools.partial(_conv_module_kernel, tt=tt),
        grid=(b, l // tt),
        in_specs=_halo_specs(tt, CV_HALO, width, colblk_v, l) + _halo_specs(tt, CV_HALO, width, colblk_g, l)
        + [pl.BlockSpec((32, width), lambda bb, i: (0, 0)), vec, vec, vec],
        out_specs=pl.BlockSpec((1, tt, width), lambda bb, i: (bb, i, 0)),
        out_shape=jax.ShapeDtypeStruct((b, l, width), BF16),
        scratch_shapes=[pltpu.VMEM((tt + 2 * CV_HALO, width), F32), pltpu.VMEM((tt, width), F32)],
        compiler_params=_cparams(2, 12 * (tt + 2 * CV_HALO) * width * 4),
        name="conv_module",
    )(u, u, u, u, u, u, wk, p["cv_b"].reshape(1, width).astype(F32),
      p["cv_ln_g"].reshape(1, width).astype(F32), p["cv_ln_b"].reshape(1, width).astype(F32))


SSD_HALO = 8
SSD_HEADS_PER_GROUP = 4
SSD_PAIR = V7X_LANES // SSD_HEAD_DIM


def _ssd_conv_kernel(x_ref, xp_ref, xn_ref, w_ref, b_ref, o_ref, pad_s, *, tt):
    _fill_padded(pad_s, x_ref[0], xp_ref[0], xn_ref[0], SSD_HALO)
    row_block = 64
    for bi, acc in enumerate(_dwconv_rows(pad_s, w_ref, SSD_CONV, SSD_HALO, tt, row_block)):
        o_ref[0, bi * row_block:bi * row_block + acc.shape[0], :] = _silu(acc + b_ref[...])


def ssd_conv(u, colblk0, ncol, width, conv_w, conv_b, tt=1024):
    b, l, _ = u.shape
    tt = min(tt, l)
    r = tt // SSD_HALO
    nh = l // SSD_HALO
    wk = jnp.zeros((V7X_SUBLANES, ncol * width), F32).at[:SSD_CONV].set(conv_w.astype(F32))
    return pl.pallas_call(
        functools.partial(_ssd_conv_kernel, tt=tt),
        grid=(b, l // tt, ncol),
        in_specs=[pl.BlockSpec((1, tt, width), lambda bb, i, c: (bb, i, colblk0 + c)),
                  pl.BlockSpec((1, SSD_HALO, width), lambda bb, i, c: (bb, jnp.maximum(i * r - 1, 0), colblk0 + c)),
                  pl.BlockSpec((1, SSD_HALO, width),
                               lambda bb, i, c: (bb, jnp.minimum((i + 1) * r, nh - 1), colblk0 + c)),
                  pl.BlockSpec((V7X_SUBLANES, width), lambda bb, i, c: (0, c)),
                  pl.BlockSpec((1, width), lambda bb, i, c: (0, c))],
        out_specs=pl.BlockSpec((1, tt, width), lambda bb, i, c: (bb, i, c)),
        out_shape=jax.ShapeDtypeStruct((b, l, ncol * width), F32),
        scratch_shapes=[pltpu.VMEM((tt + 2 * SSD_HALO, width), F32)],
        compiler_params=_cparams(3, 8 * (tt + 2 * SSD_HALO) * width * 4),
        name="ssd_conv",
    )(u, u, u, wk, conv_b.reshape(1, -1).astype(F32))


def _ssd_kernel(*refs, reverse, finish, q, gw, nheads):
    if finish:
        (xbc_ref, dtr_ref, dtb_ref, arow_ref, sel_ref, s0_ref, yf_ref, z_ref, dvec_ref, nw_ref,
         y_ref, sfin_ref, st_s, yb_s) = refs
    else:
        xbc_ref, dtr_ref, dtb_ref, arow_ref, sel_ref, s0_ref, y_ref, sfin_ref, st_s = refs
        yb_s = y_ref.at[0]

    @pl.when(pl.program_id(1) == 0)
    def _():
        st_s[...] = s0_ref[0]

    n = SSD_STATE
    dt = _softplus(dtr_ref[0] + dtb_ref[...])
    dta = dt * arow_ref[...]
    row = lax.broadcasted_iota(jnp.int32, (q, q), 0)
    col = lax.broadcasted_iota(jnp.int32, (q, q), 1)
    tri = (row <= col) if reverse else (row >= col)
    tb = tri.astype(BF16)
    d1 = dta.astype(BF16)
    rem = dta - d1.astype(F32)
    d2 = rem.astype(BF16)
    d3 = (rem - d2.astype(F32)).astype(BF16)
    cum = jnp.dot(jnp.concatenate([tb, tb, tb], axis=1), jnp.concatenate([d1, d2, d3], axis=0),
                  preferred_element_type=F32)
    cum_t = cum.T
    dt_t = dt.T
    last = cum[0:1, :] if reverse else cum[q - 1:q, :]
    last_t = cum_t[:, 0:1] if reverse else cum_t[:, q - 1:q]
    wdt_t = jnp.exp(last_t - cum_t) * dt_t
    elast = jnp.exp(last)
    ecum = jnp.exp(cum)
    e1 = ecum.astype(BF16)
    r1 = ecum - e1.astype(F32)
    e2 = r1.astype(BF16)
    e3 = (r1 - e2.astype(F32)).astype(BF16)
    ecum_wide = jnp.dot(jnp.concatenate([e1, e2, e3], axis=1), sel_ref[...], preferred_element_type=F32)
    lo_q = lax.broadcasted_iota(jnp.int32, (q, V7X_LANES), 1) < SSD_HEAD_DIM
    lo_n = lax.broadcasted_iota(jnp.int32, (n, V7X_LANES), 1) < SSD_HEAD_DIM
    b_off = nheads * SSD_HEAD_DIM
    c_off = b_off + SSD_NGROUPS * n
    for g in range(SSD_NGROUPS):
        bg = xbc_ref[0, :, b_off + g * n:b_off + (g + 1) * n]
        bg_t = bg.T
        cb = xbc_ref[0, :, c_off + g * n:c_off + (g + 1) * n].astype(BF16)
        scores = lax.dot_general(cb, bg.astype(BF16), (((1,), (1,)), ((), ())), preferred_element_type=F32)
        for pr in range(SSD_HEADS_PER_GROUP // SSD_PAIR):
            pidx = g * (SSD_HEADS_PER_GROUP // SSD_PAIR) + pr
            lanes = slice(pidx * V7X_LANES, (pidx + 1) * V7X_LANES)
            xp = xbc_ref[0, :, lanes].astype(BF16)
            yd, stn = [], []
            for hh in range(SSD_PAIR):
                h = pidx * SSD_PAIR + hh
                seg = cum[:, h:h + 1] - cum_t[h:h + 1, :]
                dec = jnp.where(tri, jnp.exp(jnp.where(tri, seg, 0.0)), 0.0)
                w = (scores * dec * dt_t[h:h + 1, :]).astype(BF16)
                yd.append(jnp.dot(w, xp, preferred_element_type=F32))
                wb = (bg_t * wdt_t[h:h + 1, :]).astype(BF16)
                stn.append(jnp.dot(wb, xp, preferred_element_type=F32))
            h0 = pidx * SSD_PAIR
            s_prev = st_s[pidx]
            y_off = jnp.dot(cb, s_prev.astype(BF16), preferred_element_type=F32) * ecum_wide[:, lanes]
            st_s[pidx] = (s_prev * jnp.where(lo_n, elast[:, h0:h0 + 1], elast[:, h0 + 1:h0 + 2])
                          + jnp.where(lo_n, stn[0], stn[1]))
            yb_s[:, lanes] = jnp.where(lo_q, yd[0], yd[1]) + y_off
    sfin_ref[0] = st_s[...]
    if finish:
        xs = xbc_ref[0, :, :b_off]
        y = yf_ref[0] + yb_s[...] + dvec_ref[...] * xs
        t = y * _silu(z_ref[0])
        for g in range(SSD_NGROUPS):
            tg = t[:, g * gw:(g + 1) * gw]
            ms = jnp.mean(tg * tg, axis=-1, keepdims=True)
            y_ref[0, :, g * gw:(g + 1) * gw] = (tg * lax.rsqrt(ms + RMS_EPS)
                                                * nw_ref[:, g * gw:(g + 1) * gw]).astype(y_ref.dtype)


def ssd_scan(xbc, dtraw, dcol, dt_bias, a_head, s0, *, reverse, fin=None, q=256):
    b, l, cw = xbc.shape
    q = min(q, l)
    nc = l // q
    nheads = dt_bias.shape[0]
    width = nheads * SSD_HEAD_DIM
    npair = nheads // SSD_PAIR
    pad = lambda v: jnp.zeros((1, V7X_LANES), F32).at[0, :nheads].set(v.astype(F32))
    cmap = (lambda bb, c: (bb, nc - 1 - c, 0)) if reverse else (lambda bb, c: (bb, c, 0))
    dmap = (lambda bb, c: (bb, nc - 1 - c, dcol)) if reverse else (lambda bb, c: (bb, c, dcol))
    full = lambda shape: pl.BlockSpec(shape, lambda *_: (0,) * len(shape))
    st_spec = pl.BlockSpec((1, npair, SSD_STATE, V7X_LANES), lambda bb, c: (bb, 0, 0, 0))
    head_of_lane = np.arange(width) // SSD_HEAD_DIM
    sel = (np.arange(V7X_LANES)[:, None] == head_of_lane[None, :]).astype(np.float32)
    sel = jnp.asarray(np.concatenate([sel, sel, sel], axis=0), BF16)
    in_specs = [pl.BlockSpec((1, q, cw), cmap), pl.BlockSpec((1, q, V7X_LANES), dmap),
                full((1, V7X_LANES)), full((1, V7X_LANES)), full(sel.shape), st_spec]
    args = [xbc, dtraw, pad(dt_bias), pad(a_head), sel, s0]
    scratch = [pltpu.VMEM((npair, SSD_STATE, V7X_LANES), F32)]
    if fin is not None:
        y_other, z, zcol, dvec, nw = fin
        zmap = (lambda bb, c: (bb, nc - 1 - c, zcol)) if reverse else (lambda bb, c: (bb, c, zcol))
        in_specs += [pl.BlockSpec((1, q, width), cmap), pl.BlockSpec((1, q, width), zmap),
                     full((1, width)), full((1, width))]
        args += [y_other, z, dvec, nw]
        scratch.append(pltpu.VMEM((q, width), F32))
    out_dtype = BF16 if fin is not None else F32
    return pl.pallas_call(
        functools.partial(_ssd_kernel, reverse=reverse, finish=fin is not None, q=q,
                          gw=width // SSD_NGROUPS, nheads=nheads),
        grid=(b, nc),
        in_specs=in_specs,
        out_specs=[pl.BlockSpec((1, q, width), cmap), st_spec],
        out_shape=[jax.ShapeDtypeStruct((b, l, width), out_dtype),
                   jax.ShapeDtypeStruct((b, npair, SSD_STATE, V7X_LANES), F32)],
        scratch_shapes=scratch,
        compiler_params=_cparams(2, 4 * q * cw * 4 + 12 * q * width * 4 + 24 * q * q * 4),
        name="ssd_scan_bwd" if reverse else "ssd_scan_fwd",
    )(*args)


def ssd_mix(u, dtraw, zcol, xcol0, width, p, s0f, s0b):
    nheads = p["ssd_dt_bias"].shape[1]
    xbc = ssd_conv(u, xcol0, 2, width, p["ssd_conv_w"], p["ssd_conv_b"])
    a_head = -jnp.exp(p["ssd_a_log"].astype(F32))
    yf, sf = ssd_scan(xbc, dtraw, 0, p["ssd_dt_bias"][0], a_head[0], s0f, reverse=False)
    dvec = jnp.repeat(p["ssd_d"].astype(F32), SSD_HEAD_DIM).reshape(1, width)
    y, sb = ssd_scan(xbc, dtraw, 1, p["ssd_dt_bias"][1], a_head[1], s0b, reverse=True,
                     fin=(yf, u, zcol, dvec, p["ssd_norm_w"].reshape(1, width).astype(F32)))
    return y, sf, sb


def _cast_pad_kernel(x_ref, o_ref, *, nvalid):
    j = pl.program_id(1)

    @pl.when(j < nvalid)
    def _():
        o_ref[...] = x_ref[...].astype(o_ref.dtype)

    @pl.when(j >= nvalid)
    def _():
        o_ref[...] = jnp.zeros_like(o_ref)


def cast_pad_bf16(w, axis, tile, total):
    depth, r, c = w.shape
    size = w.shape[axis]
    assert size % tile == 0 and total % tile == 0
    nvalid = size // tile
    if axis == 2:
        block, out_shape = (1, r, tile), (depth, r, total)
        imap = lambda l, j: (l, 0, jnp.minimum(j, nvalid - 1))
        omap = lambda l, j: (l, 0, j)
    else:
        block, out_shape = (1, tile, c), (depth, total, c)
        imap = lambda l, j: (l, jnp.minimum(j, nvalid - 1), 0)
        omap = lambda l, j: (l, j, 0)
    nelem = block[1] * block[2]
    return pl.pallas_call(
        functools.partial(_cast_pad_kernel, nvalid=nvalid),
        grid=(depth, total // tile),
        in_specs=[pl.BlockSpec(block, imap)],
        out_specs=pl.BlockSpec(block, omap),
        out_shape=jax.ShapeDtypeStruct(out_shape, BF16),
        compiler_params=_cparams(2, 2 * nelem * 6),
        name="cast_pad_bf16",
    )(w)


def _dt_weights(w_in, n_main, nheads):
    pad_last = lambda w: jnp.pad(w, ((0, 0), (0, 0), (0, V7X_LANES - w.shape[2])))
    return jnp.concatenate([pad_last(w_in[:, :, n_main:n_main + nheads]),
                            pad_last(w_in[:, :, n_main + nheads:])], axis=2)


def _mixers(hm, wts, p, s5_h0, ssd_s0, want_output=True):
    b, l, d = hm.shape
    gw = d // N_MIXERS
    hm2 = hm.reshape(b * l, d)
    u = matmul(hm2, wts["w_in"], layer=wts["layer"], n_out=wts["n_main"], tn=1024,
               name="w_in_main").reshape(b, l, -1)
    dtraw = matmul(hm2, wts["w_dt"], layer=wts["layer"], tn=2 * V7X_LANES, name="w_in_dt").reshape(b, l, -1)
    ys, s5f, s5b = s5_mix(u, 1, gw, p, *s5_h0)
    yd, sdf, sdb = ssd_mix(u, dtraw, 4, 5, gw, p, *ssd_s0)
    states = ((s5f, s5b), (sdf, sdb))
    if not want_output:
        return None, states
    ya = fnet_mix(u, 0, gw)
    yc = conv_module(u, 2, 3, gw, p)
    return [y.reshape(b * l, gw) for y in (ya, ys, yc, yd)], states


def _layer_tail(h, mix, wts, mod, mod_next, ln, *, alpha, rows_per_batch):
    (ln1_g, ln1_b, ln2_g, ln2_b) = ln
    layer = wts["layer"]
    out = matmul(mix, wts["w_out"], layer=layer, tm=512, tn=1024, name="w_out")
    h, hm = residual_layernorm(h, out, mod[2], ln1_g, ln1_b, mod[4], mod[3], alpha=alpha,
                               rows_per_batch=rows_per_batch)
    hid = matmul(hm, [wts["w_gate"], wts["w_up"]], layer=layer, tm=1024, tn=FFN_TILE, out_dtype=BF16,
                 name="gate_up")
    ff = matmul(hid, wts["w_down"], layer=layer, tm=512, tn=512, name="w_down")
    return residual_layernorm(h, ff, mod[5], ln2_g, ln2_b, mod_next[1], mod_next[0], alpha=alpha,
                              rows_per_batch=rows_per_batch)


def kernel(x, c, ctx, c_ctx, w_ada, b_ada, w_in, s5_lam_re, s5_lam_im, s5_log_dt, s5_b_re, s5_b_im,
           s5_c_re, s5_c_im, s5_d, s5_w_glu, cv_w, cv_b, cv_ln_g, cv_ln_b, ssd_conv_w, ssd_conv_b,
           ssd_a_log, ssd_dt_bias, ssd_d, ssd_norm_w, w_out, ln1_g, ln1_b, w_gate, w_up, w_down,
           ln2_g, ln2_b):
    bsz, seq, d = x.shape
    cl = ctx.shape[1]
    depth = w_in.shape[0]
    gw = d // N_MIXERS
    alpha = (2 * depth) ** 0.25
    per_layer = dict(s5_lam_re=s5_lam_re, s5_lam_im=s5_lam_im, s5_log_dt=s5_log_dt,
                     s5_b_re=s5_b_re, s5_b_im=s5_b_im, s5_c_re=s5_c_re, s5_c_im=s5_c_im, s5_d=s5_d,
                     s5_w_glu=s5_w_glu, cv_w=cv_w, cv_b=cv_b, cv_ln_g=cv_ln_g, cv_ln_b=cv_ln_b,
                     ssd_conv_w=ssd_conv_w, ssd_conv_b=ssd_conv_b, ssd_a_log=ssd_a_log,
                     ssd_dt_bias=ssd_dt_bias, ssd_d=ssd_d, ssd_norm_w=ssd_norm_w)

    c8 = jnp.zeros((V7X_SUBLANES, d), F32).at[:bsz].set(c).at[bsz].set(c_ctx)
    mods = ada_modulation(c8, w_ada, b_ada)

    def mod_vectors(i):
        m = mods[i].reshape(V7X_SUBLANES, 6, d)
        lat = [m[:bsz, k].reshape(bsz, 1, d) for k in range(6)]
        cx = [jnp.broadcast_to(m[bsz, k].reshape(1, 1, d), (bsz, 1, d)) for k in range(6)]
        return lat, cx

    nstate = (gw // V7X_LANES) * S5_STATE_BLOCK
    npair = ssd_dt_bias.shape[2] // SSD_PAIR
    z_s5 = (jnp.zeros((bsz, 1, nstate), F32), jnp.zeros((bsz, 1, nstate), F32))
    z_ssd = jnp.zeros((bsz, npair, SSD_STATE, V7X_LANES), F32)

    n_main = 4 * gw + gw + (gw + 2 * SSD_NGROUPS * SSD_STATE)
    w_dt = _dt_weights(w_in, n_main, ssd_dt_bias.shape[2])
    hidden = w_gate.shape[2]
    hidden_pad = -(-hidden // FFN_TILE) * FFN_TILE
    cast_tile = math.gcd(hidden, FFN_TILE)
    w_gate_bf = cast_pad_bf16(w_gate, 2, cast_tile, hidden_pad)
    w_up_bf = cast_pad_bf16(w_up, 2, cast_tile, hidden_pad)
    w_down_bf = cast_pad_bf16(w_down, 1, cast_tile, hidden_pad)
    lat0, cx0 = mod_vectors(0)
    h, hm = prologue(x, lat0[1], lat0[0])
    h = h.reshape(bsz * seq, d)
    hc = ctx.reshape(bsz * cl, d)
    hcm = modulate(ctx, cx0[1], cx0[0])
    for i in range(depth):
        with_ctx = i < depth - 1
        p = {k: v[i] for k, v in per_layer.items()}
        wts = dict(layer=i, n_main=n_main, w_in=w_in, w_dt=w_dt, w_out=w_out, w_gate=w_gate_bf,
                   w_up=w_up_bf, w_down=w_down_bf)
        lat, cx = mod_vectors(i)
        lat_next, cx_next = mod_vectors(min(i + 1, depth - 1))
        ln = (ln1_g[i], ln1_b[i], ln2_g[i], ln2_b[i])
        mix_c, ((c5f, c5b), (cdf, cdb)) = _mixers(hcm.reshape(bsz, cl, d), wts, p, (z_s5, z_s5),
                                                   (z_ssd, z_ssd), want_output=with_ctx)
        mix_l, _ = _mixers(hm.reshape(bsz, seq, d), wts, p, (c5f, c5b), (cdf, cdb))
        h, hm = _layer_tail(h, mix_l, wts, lat, lat_next, ln, alpha=alpha, rows_per_batch=seq)
        if with_ctx:
            hc, hcm = _layer_tail(hc, mix_c, wts, cx, cx_next, ln, alpha=alpha, rows_per_batch=cl)
    return h.reshape(bsz, seq, d).astype(x.dtype)
```

```python
import functools
import math

import jax
import jax.numpy as jnp
import numpy as np
from jax import lax
from jax.experimental import pallas as pl
from jax.experimental.pallas import tpu as pltpu

F32 = jnp.float32
BF16 = jnp.bfloat16

GRID_W = 64
N_MIXERS = 4
FNET_HEADS = 4
S5_GROUP_CH = 16
S5_STATE = 64
CONV_WIDTH = 31
SSD_HEAD_DIM = 64
SSD_NGROUPS = 4
SSD_STATE = 128
SSD_CONV = 5
LN_EPS = 1e-5
RMS_EPS = 1e-5

V7X_LANES = 128
V7X_SUBLANES = 8
V7X_VMEM_LIMIT_BYTES = 60000 * 1024
FFN_TILE = 512


def _cparams(n_axes, vmem_bytes):
    limit = int(min(max(vmem_bytes * 5 // 4 + (2 << 20), 16 << 20), V7X_VMEM_LIMIT_BYTES))
    return pltpu.CompilerParams(dimension_semantics=("arbitrary",) * n_axes,
                                vmem_limit_bytes=limit)


def _silu(x):
    return x * jax.nn.sigmoid(x)


def _gelu_tanh(x):
    return 0.5 * x * (1.0 + jnp.tanh(math.sqrt(2.0 / math.pi) * (x + 0.044715 * (x * x * x))))


def _softplus(x):
    return jnp.maximum(x, 0.0) + jnp.log1p(jnp.exp(-jnp.abs(x)))


def _ada_kernel(c_ref, w_ref, b_ref, o_ref):
    cs = _silu(c_ref[...])
    ch = cs.astype(BF16)
    cl = (cs - ch.astype(F32)).astype(BF16)
    w = w_ref[0]
    wh = w.astype(BF16)
    wl = (w - wh.astype(F32)).astype(BF16)
    rows = cs.shape[0]
    top = jnp.dot(jnp.concatenate([ch, cl], axis=0), wh, preferred_element_type=F32)
    o_ref[0] = top[:rows] + top[rows:] + jnp.dot(ch, wl, preferred_element_type=F32) + b_ref[0]


def ada_modulation(c8, w_ada, b_ada, tn=1024):
    depth, d, n = w_ada.shape
    tn = min(tn, n)
    return pl.pallas_call(
        _ada_kernel,
        grid=(depth, n // tn),
        in_specs=[pl.BlockSpec((V7X_SUBLANES, d), lambda l, j: (0, 0)),
                  pl.BlockSpec((1, d, tn), lambda l, j: (l, 0, j)),
                  pl.BlockSpec((1, 1, tn), lambda l, j: (l, 0, j))],
        out_specs=pl.BlockSpec((1, V7X_SUBLANES, tn), lambda l, j: (l, 0, j)),
        out_shape=jax.ShapeDtypeStruct((depth, V7X_SUBLANES, n), F32),
        compiler_params=_cparams(2, 2 * d * tn * 4 + 6 * d * tn * 2),
        name="ada_modulation",
    )(c8, w_ada, b_ada.reshape(depth, 1, n))


def _prologue_kernel(x_ref, tr_ref, tc_ref, sc_ref, sh_ref, h_ref, hm_ref, *, tp, half):
    sc = 1.0 + sc_ref[0]
    sh = sh_ref[0]
    for rr in range(tp // GRID_W):
        rows = slice(rr * GRID_W, (rr + 1) * GRID_W)
        lo = x_ref[0, rows, :half] + tr_ref[rr:rr + 1, :]
        hi = x_ref[0, rows, half:] + tc_ref[...]
        h_ref[0, rows, :half] = lo
        h_ref[0, rows, half:] = hi
        hm_ref[0, rows, :half] = (lo * sc[:, :half] + sh[:, :half]).astype(BF16)
        hm_ref[0, rows, half:] = (hi * sc[:, half:] + sh[:, half:]).astype(BF16)


def prologue(x, sc, sh, tp=512):
    b, l, d = x.shape
    tp = min(tp, l)
    half = d // 2
    q = d // 4
    omega = jnp.exp(-math.log(10000.0) * jnp.arange(q, dtype=F32) / q)[None]
    r = jnp.arange(l // GRID_W, dtype=F32)[:, None]
    col = jnp.arange(GRID_W, dtype=F32)[:, None]
    tab_r = jnp.concatenate([jnp.sin(r * omega), jnp.cos(r * omega)], -1)
    tab_c = jnp.concatenate([jnp.sin(col * omega), jnp.cos(col * omega)], -1)
    rpt = tp // GRID_W
    return pl.pallas_call(
        functools.partial(_prologue_kernel, tp=tp, half=half),
        grid=(b, l // tp),
        in_specs=[pl.BlockSpec((1, tp, d), lambda bb, i: (bb, i, 0)),
                  pl.BlockSpec((rpt, half), lambda bb, i: (i, 0)),
                  pl.BlockSpec((GRID_W, half), lambda bb, i: (0, 0)),
                  pl.BlockSpec((1, 1, d), lambda bb, i: (bb, 0, 0)),
                  pl.BlockSpec((1, 1, d), lambda bb, i: (bb, 0, 0))],
        out_specs=[pl.BlockSpec((1, tp, d), lambda bb, i: (bb, i, 0)),
                   pl.BlockSpec((1, tp, d), lambda bb, i: (bb, i, 0))],
        out_shape=[jax.ShapeDtypeStruct((b, l, d), F32), jax.ShapeDtypeStruct((b, l, d), BF16)],
        compiler_params=_cparams(2, 2 * tp * d * (4 + 4 + 2)),
        name="prologue",
    )(x, tab_r, tab_c, sc, sh)


def _modulate_kernel(x_ref, sc_ref, sh_ref, o_ref):
    o_ref[0] = (x_ref[0] * (1.0 + sc_ref[0]) + sh_ref[0]).astype(BF16)


def modulate(x, sc, sh, tp=256):
    b, l, d = x.shape
    tp = min(tp, l)
    return pl.pallas_call(
        _modulate_kernel,
        grid=(b, l // tp),
        in_specs=[pl.BlockSpec((1, tp, d), lambda bb, i: (bb, i, 0)),
                  pl.BlockSpec((1, 1, d), lambda bb, i: (bb, 0, 0)),
                  pl.BlockSpec((1, 1, d), lambda bb, i: (bb, 0, 0))],
        out_specs=pl.BlockSpec((1, tp, d), lambda bb, i: (bb, i, 0)),
        out_shape=jax.ShapeDtypeStruct((b, l, d), BF16),
        compiler_params=_cparams(2, 2 * tp * d * 6),
        name="modulate",
    )(x, sc, sh)


def _mm_kernel(*refs, n_lhs, n_w, cast_w):
    x_refs = refs[:n_lhs]
    w_refs = refs[n_lhs:n_lhs + n_w]
    o_ref = refs[n_lhs + n_w]
    if cast_w:
        wb = refs[n_lhs + n_w + 1:]

        @pl.when(pl.program_id(1) == 0)
        def _():
            for w_ref, s in zip(w_refs, wb):
                s[...] = w_ref[...].astype(BF16)

        ws = [s[...] for s in wb]
    else:
        ws = [w_ref[...] for w_ref in w_refs]
    xs = [x_ref[...].astype(BF16) for x_ref in x_refs]
    x = xs[0] if n_lhs == 1 else jnp.concatenate(xs, axis=1)
    acc = [jnp.dot(x, w, preferred_element_type=F32) for w in ws]
    y = acc[0] if n_w == 1 else _silu(acc[0]) * acc[1]
    o_ref[...] = y.astype(o_ref.dtype)


def matmul(xs, ws, *, layer, n_out=None, tm=1024, tn=512, out_dtype=F32, name="matmul"):
    slabs = list(xs) if isinstance(xs, (list, tuple)) else [xs]
    wl = list(ws) if isinstance(ws, (list, tuple)) else [ws]
    m = slabs[0].shape[0]
    kk, n = wl[0].shape[-2:]
    n = n if n_out is None else n_out
    tm, tn = min(tm, m), min(tn, n)
    assert m % tm == 0 and n % tn == 0
    cast_w = wl[0].dtype != BF16
    x_specs = [pl.BlockSpec((tm, s.shape[1]), lambda j, i: (i, 0)) for s in slabs]
    if cast_w:
        mode = dict(pipeline_mode=pl.Buffered(1)) if m > tm else {}
        w_specs = [pl.BlockSpec((None, kk, tn), lambda j, i: (layer, 0, j), **mode) for _ in wl]
        scratch = [pltpu.VMEM((kk, tn), BF16) for _ in wl]
    else:
        w_specs = [pl.BlockSpec((None, kk, tn), lambda j, i: (layer, 0, j)) for _ in wl]
        scratch = []
    ob = jnp.dtype(out_dtype).itemsize
    xbytes = sum(tm * s.shape[1] * s.dtype.itemsize for s in slabs)
    wbytes = len(wl) * kk * tn * (((4 if m > tm else 2 * 4) + 2) if cast_w else 2 * 2)
    return pl.pallas_call(
        functools.partial(_mm_kernel, n_lhs=len(slabs), n_w=len(wl), cast_w=cast_w),
        grid=(n // tn, m // tm),
        in_specs=x_specs + w_specs,
        out_specs=pl.BlockSpec((tm, tn), lambda j, i: (i, j)),
        out_shape=jax.ShapeDtypeStruct((m, n), out_dtype),
        scratch_shapes=scratch,
        compiler_params=_cparams(2, 2 * xbytes + wbytes + 2 * tm * tn * ob + (len(wl) + 1) * tm * tn * 4
                                 + (tm * kk * 2 if len(slabs) > 1 else 0)),
        name=name,
    )(*slabs, *wl)


def _ln_kernel(h_ref, mix_ref, g_ref, lng_ref, lnb_ref, sc_ref, sh_ref, ho_ref, hm_ref, *, alpha):
    r = alpha * h_ref[...] + g_ref[0] * mix_ref[...]
    mu = jnp.mean(r, axis=-1, keepdims=True)
    rc = r - mu
    var = jnp.mean(rc * rc, axis=-1, keepdims=True)
    y = rc * lax.rsqrt(var + LN_EPS) * lng_ref[...] + lnb_ref[...]
    ho_ref[...] = y
    hm_ref[...] = (y * (1.0 + sc_ref[0]) + sh_ref[0]).astype(BF16)


def residual_layernorm(h, mix, gate, ln_g, ln_b, sc, sh, *, alpha, rows_per_batch, tm=256):
    m, n = h.shape
    tm = min(tm, m, rows_per_batch)
    row = pl.BlockSpec((tm, n), lambda i: (i, 0))
    vec_b = pl.BlockSpec((1, 1, n), lambda i: ((i * tm) // rows_per_batch, 0, 0))
    vec = pl.BlockSpec((1, n), lambda i: (0, 0))
    return pl.pallas_call(
        functools.partial(_ln_kernel, alpha=alpha),
        grid=(m // tm,),
        in_specs=[row, row, vec_b, vec, vec, vec_b, vec_b],
        out_specs=[row, row],
        out_shape=[jax.ShapeDtypeStruct((m, n), F32), jax.ShapeDtypeStruct((m, n), BF16)],
        compiler_params=_cparams(1, 2 * tm * n * (4 + 4 + 4 + 2) + 4 * tm * n * 4),
        name="residual_layernorm",
    )(h, mix, gate, ln_g.reshape(1, n), ln_b.reshape(1, n), sc, sh)


FNET_N2 = 128


def _dft_cos_sin(n):
    k = np.arange(n, dtype=np.int64)
    ang = (2.0 * np.pi / n) * ((k[:, None] * k[None, :]) % n).astype(np.float64)
    return np.cos(ang), np.sin(ang)


def _split_bf16(m):
    m = jnp.asarray(m, F32)
    hi = m.astype(BF16)
    return hi, (m - hi.astype(F32)).astype(BF16)


def _const_lhs3(m):
    hi, lo = _split_bf16(m)
    return jnp.concatenate([hi, hi, lo], axis=1)


def _const_rhs3(m):
    hi, lo = _split_bf16(m)
    return jnp.concatenate([hi, hi, lo], axis=0)


def _dot3_const_lhs(m3, x):
    xh = x.astype(BF16)
    xl = (x - xh.astype(F32)).astype(BF16)
    return jnp.dot(m3, jnp.concatenate([xh, xl, xh], axis=0), preferred_element_type=F32)


def _dot3_const_rhs(x, r3):
    xh = x.astype(BF16)
    xl = (x - xh.astype(F32)).astype(BF16)
    return jnp.dot(jnp.concatenate([xh, xl, xh], axis=1), r3, preferred_element_type=F32)


def _channel_dft(x, cs3_ref, hd):
    ps, qs = [], []
    for hh in range(x.shape[1] // hd):
        pq = _dot3_const_rhs(x[:, hh * hd:(hh + 1) * hd], cs3_ref[...])
        ps.append(pq[:, :hd])
        qs.append(pq[:, hd:])
    return jnp.concatenate(ps, axis=1), jnp.concatenate(qs, axis=1)


def _fnet_a_kernel(x_ref, cs3_ref, m1_ref, twr_ref, twi_ref, ar_ref, ai_ref, *, hd, n1):
    nsub = x_ref.shape[2]
    x = jnp.concatenate([x_ref[0, :, j, :] for j in range(nsub)], axis=0)
    p, q = _channel_dft(x, cs3_ref, hd)
    reps = x.shape[1] // V7X_LANES
    for j in range(nsub):
        rows = slice(j * n1, (j + 1) * n1)
        mp = _dot3_const_lhs(m1_ref[...], p[rows])
        mq = _dot3_const_lhs(m1_ref[...], q[rows])
        ar = mp[:n1] - mq[n1:]
        ai = -(mq[:n1] + mp[n1:])
        tr = jnp.tile(twr_ref[j], (1, reps))
        ti = jnp.tile(twi_ref[j], (1, reps))
        ar_ref[0, j] = ar * tr - ai * ti
        ai_ref[0, j] = ar * ti + ai * tr


def _fnet_c_kernel(ar_ref, ai_ref, m2_ref, o_ref, *, scale):
    for j in range(ar_ref.shape[2]):
        z = jnp.concatenate([ar_ref[0, :, j, :], ai_ref[0, :, j, :]], axis=0)
        o_ref[0, :, j, :] = _dot3_const_lhs(m2_ref[...], z) * scale


def _fnet_dense_kernel(x_ref, cs3_ref, ml_ref, o_ref, *, hd, scale):
    p, q = _channel_dft(x_ref[0], cs3_ref, hd)
    o_ref[0] = _dot3_const_lhs(ml_ref[...], jnp.concatenate([p, q], axis=0)) * scale


def fnet_mix(u, colblk, width):
    b, l, ctot = u.shape
    hd = width // FNET_HEADS
    scale = 1.0 / math.sqrt(l * hd)
    cd, sd = _dft_cos_sin(hd)
    cs3 = _const_rhs3(np.concatenate([cd, sd], axis=1))
    full = lambda shape: pl.BlockSpec(shape, lambda *_: (0,) * len(shape))
    n2 = FNET_N2
    if l < V7X_SUBLANES * n2:
        cl, sl = _dft_cos_sin(l)
        ml = _const_lhs3(np.concatenate([cl, -sl], axis=1))
        return pl.pallas_call(
            functools.partial(_fnet_dense_kernel, hd=hd, scale=scale),
            grid=(b,),
            in_specs=[pl.BlockSpec((1, l, width), lambda bb: (bb, 0, colblk)), full(cs3.shape), full(ml.shape)],
            out_specs=pl.BlockSpec((1, l, width), lambda bb: (bb, 0, 0)),
            out_shape=jax.ShapeDtypeStruct((b, l, width), F32),
            compiler_params=_cparams(1, 16 * l * width * 4),
            name="fnet_dense",
        )(u, cs3, ml)
    n1 = l // n2
    nsub = V7X_SUBLANES
    c1, s1 = _dft_cos_sin(n1)
    c2, s2 = _dft_cos_sin(n2)
    m1 = _const_lhs3(np.concatenate([c1, s1], axis=0))
    m2 = _const_lhs3(np.concatenate([c2, s2], axis=1))
    kk = (np.arange(n2, dtype=np.int64)[:, None] * np.arange(n1, dtype=np.int64)[None, :]) % l
    ang = (2.0 * np.pi / l) * kk.astype(np.float64)
    twr = jnp.asarray(np.broadcast_to(np.cos(ang)[:, :, None], (n2, n1, V7X_LANES)), F32)
    twi = jnp.asarray(np.broadcast_to(-np.sin(ang)[:, :, None], (n2, n1, V7X_LANES)), F32)
    ar, ai = pl.pallas_call(
        functools.partial(_fnet_a_kernel, hd=hd, n1=n1),
        grid=(b, n2 // nsub),
        in_specs=[pl.BlockSpec((1, n1, nsub, width), lambda bb, j: (bb, 0, j, colblk)),
                  full(cs3.shape), full(m1.shape),
                  pl.BlockSpec((nsub, n1, V7X_LANES), lambda bb, j: (j, 0, 0)),
                  pl.BlockSpec((nsub, n1, V7X_LANES), lambda bb, j: (j, 0, 0))],
        out_specs=[pl.BlockSpec((1, nsub, n1, width), lambda bb, j: (bb, j, 0, 0))] * 2,
        out_shape=[jax.ShapeDtypeStruct((b, n2, n1, width), F32)] * 2,
        compiler_params=_cparams(2, 40 * nsub * n1 * width * 4),
        name="fnet_stage_a",
    )(u.reshape(b, n1, n2, ctot), cs3, m1, twr, twi)
    y = pl.pallas_call(
        functools.partial(_fnet_c_kernel, scale=scale),
        grid=(b, n1 // nsub),
        in_specs=[pl.BlockSpec((1, n2, nsub, width), lambda bb, j: (bb, 0, j, 0)),
                  pl.BlockSpec((1, n2, nsub, width), lambda bb, j: (bb, 0, j, 0)),
                  full(m2.shape)],
        out_specs=pl.BlockSpec((1, n2, nsub, width), lambda bb, j: (bb, 0, j, 0)),
        out_shape=jax.ShapeDtypeStruct((b, n2, n1, width), F32),
        compiler_params=_cparams(2, 10 * n2 * nsub * width * 4),
        name="fnet_stage_c",
    )(ar, ai, m2)
    return y.reshape(b, l, width)


S5_LANE_BLOCK_GROUPS = V7X_LANES // S5_GROUP_CH
S5_STATE_BLOCK = S5_LANE_BLOCK_GROUPS * S5_STATE
S5_LEVELS = (1, 2, 4)
S5_SEGMENTS = V7X_SUBLANES


def _s5_direction_params(p, k, reverse, tseg):
    g, st = p["s5_lam_re"].shape[1:]
    lr, li = p["s5_lam_re"][k].astype(F32), p["s5_lam_im"][k].astype(F32)
    dt = jnp.exp(p["s5_log_dt"][k].astype(F32))[:, None]
    zr, zi = lr * dt, li * dt

    def power(s):
        e = jnp.exp(s * zr)
        return e * jnp.cos(s * zi), e * jnp.sin(s * zi)

    ar, ai = power(1.0)
    nr, ni = ar - 1.0, ai
    den = lr * lr + li * li
    cr, ci = (nr * lr + ni * li) / den, (ni * lr - nr * li) / den
    b_re, b_im = p["s5_b_re"].astype(F32), p["s5_b_im"].astype(F32)
    btr = b_re * cr[..., None] - b_im * ci[..., None]
    bti = b_re * ci[..., None] + b_im * cr[..., None]
    nj = g // S5_LANE_BLOCK_GROUPS
    eye = jnp.eye(S5_LANE_BLOCK_GROUPS, dtype=F32)

    def blockdiag_b(bt):
        b4 = bt.reshape(nj, S5_LANE_BLOCK_GROUPS, st, S5_GROUP_CH)
        return jnp.einsum("jgpc,gh->jgchp", b4, eye).reshape(nj, V7X_LANES, S5_STATE_BLOCK).astype(BF16)

    t = jnp.arange(V7X_SUBLANES)
    tabs_r, tabs_i = [], []
    for s in S5_LEVELS:
        pr, pi = power(float(s * tseg))
        keep = (t <= V7X_SUBLANES - 1 - s) if reverse else (t >= s)
        tabs_r.append(jnp.where(keep[:, None], pr.reshape(1, -1), 0.0))
        tabs_i.append(jnp.where(keep[:, None], pi.reshape(1, -1), 0.0))
    steps = ((V7X_SUBLANES - t) if reverse else (t + 1))[:, None, None].astype(F32) * float(tseg)
    e = jnp.exp(steps * zr[None])
    tabs_r.append((e * jnp.cos(steps * zi[None])).reshape(V7X_SUBLANES, -1))
    tabs_i.append((e * jnp.sin(steps * zi[None])).reshape(V7X_SUBLANES, -1))
    tabs_r.append(jnp.broadcast_to(ar.reshape(1, -1), (V7X_SUBLANES, ar.size)))
    tabs_i.append(jnp.broadcast_to(ai.reshape(1, -1), (V7X_SUBLANES, ai.size)))
    return blockdiag_b(btr), blockdiag_b(bti), jnp.stack(tabs_r), jnp.stack(tabs_i)


def _s5_output_params(p):
    g, ch, st = p["s5_c_re"].shape
    nj = g // S5_LANE_BLOCK_GROUPS
    eye = jnp.eye(S5_LANE_BLOCK_GROUPS, dtype=F32)

    def blockdiag_c(c):
        c4 = c.astype(F32).reshape(nj, S5_LANE_BLOCK_GROUPS, ch, st)
        return jnp.einsum("jgcp,gh->jgphc", c4, eye).reshape(nj, S5_STATE_BLOCK, V7X_LANES).astype(BF16)

    return blockdiag_c(p["s5_c_re"]), blockdiag_c(p["s5_c_im"])


def _s5_kernel(*refs, reverse, finish, tseg, nj):
    if finish:
        (u_ref, h0r_ref, h0i_ref, bre_ref, bim_ref, cre_ref, cim_ref, tr_ref, ti_ref,
         yf_ref, d_ref, wglu_ref, y_ref, hfr_ref, hfi_ref,
         up_s, xr_s, xi_s, hr_s, hi_s, car_s, cai_s, yp_s, yt_s) = refs
    else:
        (u_ref, h0r_ref, h0i_ref, bre_ref, bim_ref, cre_ref, cim_ref, tr_ref, ti_ref,
         y_ref, hfr_ref, hfi_ref, up_s, xr_s, xi_s, hr_s, hi_s, car_s, cai_s) = refs

    @pl.when(pl.program_id(1) == 0)
    def _():
        car_s[...] = h0r_ref[0]
        cai_s[...] = h0i_ref[0]

    nseg = S5_SEGMENTS
    sb = S5_STATE_BLOCK
    tile = lambda t: pl.ds(pl.multiple_of(t * nseg, nseg), nseg)

    def gather(t, c):
        up_s[tile(t), :] = u_ref[0, :, t, :]
        return c

    lax.fori_loop(0, tseg, gather, 0)
    edge = 0 if reverse else nseg - 1
    first = nseg - 1 if reverse else 0
    for j in range(nj):
        lanes = slice(j * V7X_LANES, (j + 1) * V7X_LANES)
        states = slice(j * sb, (j + 1) * sb)
        ub = up_s[:, lanes].astype(BF16)
        xr_s[...] = jnp.dot(ub, bre_ref[j], preferred_element_type=F32)
        xi_s[...] = jnp.dot(ub, bim_ref[j], preferred_element_type=F32)
        ar = tr_ref[len(S5_LEVELS) + 1, :, states]
        ai = ti_ref[len(S5_LEVELS) + 1, :, states]

        def step(t, carry, store, ar=ar, ai=ai):
            hr, hi = carry
            rows = tile((tseg - 1 - t) if reverse else t)
            hr, hi = ar * hr - ai * hi + xr_s[rows, :], ar * hi + ai * hr + xi_s[rows, :]
            if store:
                hr_s[rows, :] = hr
                hi_s[rows, :] = hi
            return hr, hi

        zero = jnp.zeros((nseg, sb), F32)
        gr, gi = lax.fori_loop(0, tseg, functools.partial(step, store=False), (zero, zero), unroll=4)
        for lvl, s in enumerate(S5_LEVELS):
            pr = tr_ref[lvl, :, states]
            pi = ti_ref[lvl, :, states]
            shift = (nseg - s) if reverse else s
            sr = pltpu.roll(gr, shift, 0)
            si = pltpu.roll(gi, shift, 0)
            gr, gi = gr + pr * sr - pi * si, gi + pr * si + pi * sr
        pr = tr_ref[len(S5_LEVELS), :, states]
        pi = ti_ref[len(S5_LEVELS), :, states]
        cr = jnp.broadcast_to(car_s[:, states], (nseg, sb))
        ci = jnp.broadcast_to(cai_s[:, states], (nseg, sb))
        gr, gi = gr + pr * cr - pi * ci, gi + pr * ci + pi * cr
        car_s[:, states] = gr[edge:edge + 1, :]
        cai_s[:, states] = gi[edge:edge + 1, :]
        row = lax.broadcasted_iota(jnp.int32, (nseg, sb), 0)
        shift = (nseg - 1) if reverse else 1
        start = (jnp.where(row == first, cr, pltpu.roll(gr, shift, 0)),
                 jnp.where(row == first, ci, pltpu.roll(gi, shift, 0)))
        lax.fori_loop(0, tseg, functools.partial(step, store=True), start, unroll=4)
        yp = (jnp.dot(hr_s[...].astype(BF16), cre_ref[j], preferred_element_type=F32)
              - jnp.dot(hi_s[...].astype(BF16), cim_ref[j], preferred_element_type=F32))
        if finish:
            yp_s[:, lanes] = yp + yf_ref[0, :, lanes]
        else:
            y_ref[0, :, lanes] = yp

    hfr_ref[0] = car_s[...]
    hfi_ref[0] = cai_s[...]
    if finish:
        def scatter(t, c):
            yt_s[:, t, :] = yp_s[tile(t), :]
            return c

        lax.fori_loop(0, tseg, scatter, 0)
        for s in range(nseg):
            y = yt_s[s] + d_ref[...] * u_ref[0, s]
            gl = _gelu_tanh(y)
            gate = jnp.dot(gl.astype(BF16), wglu_ref[...], preferred_element_type=F32)
            y_ref[0, s * tseg:(s + 1) * tseg, :] = (gl * jax.nn.sigmoid(gate)).astype(y_ref.dtype)


def s5_scan(u, colblk, width, dirp, outp, h0, *, reverse, tseg, fin=None):
    b, l, ctot = u.shape
    nseg = S5_SEGMENTS
    tt = nseg * tseg
    nc = l // tt
    nj = width // V7X_LANES
    ns = nj * S5_STATE_BLOCK
    bre, bim, tr, ti = dirp
    cre, cim = outp
    h0r, h0i = h0
    cmap = (lambda bb, c: (bb, nc - 1 - c, 0, colblk)) if reverse else (lambda bb, c: (bb, c, 0, colblk))
    omap = (lambda bb, c: (bb, nc - 1 - c, 0)) if reverse else (lambda bb, c: (bb, c, 0))
    full = lambda shape: pl.BlockSpec(shape, lambda *_: (0,) * len(shape))
    st_spec = pl.BlockSpec((1, 1, ns), lambda bb, c: (bb, 0, 0))
    in_specs = [pl.BlockSpec((1, nseg, tseg, width), cmap), st_spec, st_spec,
                full(bre.shape), full(bim.shape), full(cre.shape), full(cim.shape),
                full(tr.shape), full(ti.shape)]
    args = [u.reshape(b, l // tseg, tseg, ctot), h0r, h0i, bre, bim, cre, cim, tr, ti]
    state_tile = pltpu.VMEM((tt, S5_STATE_BLOCK), F32)
    scratch = [pltpu.VMEM((tt, width), F32), state_tile, state_tile, state_tile, state_tile,
               pltpu.VMEM((1, ns), F32), pltpu.VMEM((1, ns), F32)]
    vmem = 3 * tt * width * 4 + 4 * tt * S5_STATE_BLOCK * 4 + 2 * tt * width * 4
    if fin is not None:
        y_other, d, wglu = fin
        in_specs += [pl.BlockSpec((1, tt, width), omap), full((1, width)), full(wglu.shape)]
        args += [y_other, d.reshape(1, width).astype(F32), wglu]
        scratch += [pltpu.VMEM((tt, width), F32), pltpu.VMEM((nseg, tseg, width), F32)]
        vmem += 4 * tt * width * 4 + 2 * wglu.size * 2
    out_dtype = BF16 if fin is not None else F32
    vmem += 2 * 4 * bre.size * 2 + 2 * 2 * tr.size * 4
    return pl.pallas_call(
        functools.partial(_s5_kernel, reverse=reverse, finish=fin is not None, tseg=tseg, nj=nj),
        grid=(b, nc),
        in_specs=in_specs,
        out_specs=[pl.BlockSpec((1, tt, width), omap), st_spec, st_spec],
        out_shape=[jax.ShapeDtypeStruct((b, l, width), out_dtype),
                   jax.ShapeDtypeStruct((b, 1, ns), F32), jax.ShapeDtypeStruct((b, 1, ns), F32)],
        scratch_shapes=scratch,
        compiler_params=_cparams(2, vmem),
        name="s5_scan_bwd" if reverse else "s5_scan_fwd",
    )(*args)


def s5_mix(u, colblk, width, p, h0f, h0b, tseg=128):
    tseg = min(tseg, u.shape[1] // S5_SEGMENTS)
    outp = _s5_output_params(p)
    yf, hf_r, hf_i = s5_scan(u, colblk, width, _s5_direction_params(p, 0, False, tseg), outp, h0f,
                             reverse=False, tseg=tseg)
    y, hb_r, hb_i = s5_scan(u, colblk, width, _s5_direction_params(p, 1, True, tseg), outp, h0b,
                            reverse=True, tseg=tseg, fin=(yf, p["s5_d"], p["s5_w_glu"].astype(BF16)))
    return y, (hf_r, hf_i), (hb_r, hb_i)


def _fill_padded(pad_s, main, prev, nxt, halo):
    i = pl.program_id(1)
    n = pl.num_programs(1)
    tt = main.shape[0]
    pad_s[0:halo, :] = jnp.where(i > 0, prev, 0.0)
    pad_s[halo:halo + tt, :] = main
    pad_s[halo + tt:halo + tt + halo, :] = jnp.where(i < n - 1, nxt, 0.0)


def _dwconv_rows(pad_s, w_ref, width, halo, tt, row_block=32):
    off = halo - (width - 1) // 2
    outs = []
    for r0 in range(0, tt, row_block):
        rb = min(row_block, tt - r0)
        acc = None
        for k in range(width):
            term = w_ref[k:k + 1, :] * pad_s[r0 + off + k:r0 + off + k + rb, :]
            acc = term if acc is None else acc + term
        outs.append(acc)
    return outs


def _halo_specs(tt, halo, width, colblk, l):
    r = tt // halo
    nh = l // halo
    main = pl.BlockSpec((1, tt, width), lambda bb, i: (bb, i, colblk))
    prev = pl.BlockSpec((1, halo, width), lambda bb, i: (bb, jnp.maximum(i * r - 1, 0), colblk))
    nxt = pl.BlockSpec((1, halo, width), lambda bb, i: (bb, jnp.minimum((i + 1) * r, nh - 1), colblk))
    return [main, prev, nxt]


CV_HALO = 16


CV_ROW_BLOCK = 64


def _conv_module_kernel(v_ref, vp_ref, vn_ref, g_ref, gp_ref, gn_ref, w_ref, b_ref, lg_ref, lb_ref,
                        o_ref, pad_s, acc_s, *, tt):
    glu = lambda v, g: v * jax.nn.sigmoid(g)
    _fill_padded(pad_s, glu(v_ref[0], g_ref[0]), glu(vp_ref[0], gp_ref[0]), glu(vn_ref[0], gn_ref[0]), CV_HALO)
    sub, rb = V7X_SUBLANES, CV_ROW_BLOCK
    off = CV_HALO - (CONV_WIDTH - 1) // 2

    def row_block(bi, c):
        r0 = pl.multiple_of(bi * rb, rb)
        for lb in range(acc_s.shape[1] // V7X_LANES):
            lanes = slice(lb * V7X_LANES, (lb + 1) * V7X_LANES)
            y = None
            for r in range(sub):
                part = None
                for m in range(off, off + CONV_WIDTH):
                    if m % sub != r:
                        continue
                    rows = pl.ds(pl.multiple_of(r0 + (m // sub) * sub, sub), rb + sub)
                    term = w_ref[m - off:m - off + 1, lanes] * pad_s[rows, lanes]
                    part = term if part is None else part + term
                if part is not None:
                    y = part[r:r + rb, :] if y is None else y + part[r:r + rb, :]
            acc_s[pl.ds(r0, rb), lanes] = y
        hcv = acc_s[pl.ds(r0, rb), :] + b_ref[...]
        mu = jnp.mean(hcv, axis=-1, keepdims=True)
        hc = hcv - mu
        var = jnp.mean(hc * hc, axis=-1, keepdims=True)
        yn = hc * lax.rsqrt(var + LN_EPS) * lg_ref[...] + lb_ref[...]
        o_ref[0, pl.ds(r0, rb), :] = _silu(yn).astype(o_ref.dtype)
        return c

    lax.fori_loop(0, tt // rb, row_block, 0)


def conv_module(u, colblk_v, colblk_g, width, p, tt=256):
    b, l, _ = u.shape
    tt = min(tt, l)
    assert tt % CV_ROW_BLOCK == 0
    wk = jnp.zeros((32, width), F32).at[:CONV_WIDTH].set(p["cv_w"].astype(F32))
    vec = pl.BlockSpec((1, width), lambda bb, i: (0, 0))
    return pl.pallas_call(
        functools.partial(_conv_module_kernel, tt=tt),
        grid=(b, l // tt),
        in_specs=_halo_specs(tt, CV_HALO, width, colblk_v, l) + _halo_specs(tt, CV_HALO, width, colblk_g, l)
        + [pl.BlockSpec((32, width), lambda bb, i: (0, 0)), vec, vec, vec],
        out_specs=pl.BlockSpec((1, tt, width), lambda bb, i: (bb, i, 0)),
        out_shape=jax.ShapeDtypeStruct((b, l, width), BF16),
        scratch_shapes=[pltpu.VMEM((tt + 2 * CV_HALO, width), F32), pltpu.VMEM((tt, width), F32)],
        compiler_params=_cparams(2, 12 * (tt + 2 * CV_HALO) * width * 4),
        name="conv_module",
    )(u, u, u, u, u, u, wk, p["cv_b"].reshape(1, width).astype(F32),
      p["cv_ln_g"].reshape(1, width).astype(F32), p["cv_ln_b"].reshape(1, width).astype(F32))


SSD_HALO = 8
SSD_HEADS_PER_GROUP = 4
SSD_PAIR = V7X_LANES // SSD_HEAD_DIM


def _ssd_conv_kernel(x_ref, xp_ref, xn_ref, w_ref, b_ref, o_ref, pad_s, *, tt):
    _fill_padded(pad_s, x_ref[0], xp_ref[0], xn_ref[0], SSD_HALO)
    row_block = 64
    for bi, acc in enumerate(_dwconv_rows(pad_s, w_ref, SSD_CONV, SSD_HALO, tt, row_block)):
        o_ref[0, bi * row_block:bi * row_block + acc.shape[0], :] = _silu(acc + b_ref[...])


def ssd_conv(u, colblk0, ncol, width, conv_w, conv_b, tt=1024):
    b, l, _ = u.shape
    tt = min(tt, l)
    r = tt // SSD_HALO
    nh = l // SSD_HALO
    wk = jnp.zeros((V7X_SUBLANES, ncol * width), F32).at[:SSD_CONV].set(conv_w.astype(F32))
    return pl.pallas_call(
        functools.partial(_ssd_conv_kernel, tt=tt),
        grid=(b, l // tt, ncol),
        in_specs=[pl.BlockSpec((1, tt, width), lambda bb, i, c: (bb, i, colblk0 + c)),
                  pl.BlockSpec((1, SSD_HALO, width), lambda bb, i, c: (bb, jnp.maximum(i * r - 1, 0), colblk0 + c)),
                  pl.BlockSpec((1, SSD_HALO, width),
                               lambda bb, i, c: (bb, jnp.minimum((i + 1) * r, nh - 1), colblk0 + c)),
                  pl.BlockSpec((V7X_SUBLANES, width), lambda bb, i, c: (0, c)),
                  pl.BlockSpec((1, width), lambda bb, i, c: (0, c))],
        out_specs=pl.BlockSpec((1, tt, width), lambda bb, i, c: (bb, i, c)),
        out_shape=jax.ShapeDtypeStruct((b, l, ncol * width), F32),
        scratch_shapes=[pltpu.VMEM((tt + 2 * SSD_HALO, width), F32)],
        compiler_params=_cparams(3, 8 * (tt + 2 * SSD_HALO) * width * 4),
        name="ssd_conv",
    )(u, u, u, wk, conv_b.reshape(1, -1).astype(F32))


def _ssd_kernel(*refs, reverse, finish, q, gw, nheads):
    if finish:
        (xbc_ref, dtr_ref, dtb_ref, arow_ref, sel_ref, s0_ref, yf_ref, z_ref, dvec_ref, nw_ref,
         y_ref, sfin_ref, st_s, yb_s) = refs
    else:
        xbc_ref, dtr_ref, dtb_ref, arow_ref, sel_ref, s0_ref, y_ref, sfin_ref, st_s = refs
        yb_s = y_ref.at[0]

    @pl.when(pl.program_id(1) == 0)
    def _():
        st_s[...] = s0_ref[0]

    n = SSD_STATE
    dt = _softplus(dtr_ref[0] + dtb_ref[...])
    dta = dt * arow_ref[...]
    row = lax.broadcasted_iota(jnp.int32, (q, q), 0)
    col = lax.broadcasted_iota(jnp.int32, (q, q), 1)
    tri = (row <= col) if reverse else (row >= col)
    tb = tri.astype(BF16)
    d1 = dta.astype(BF16)
    rem = dta - d1.astype(F32)
    d2 = rem.astype(BF16)
    d3 = (rem - d2.astype(F32)).astype(BF16)
    cum = jnp.dot(jnp.concatenate([tb, tb, tb], axis=1), jnp.concatenate([d1, d2, d3], axis=0),
                  preferred_element_type=F32)
    cum_t = cum.T
    dt_t = dt.T
    last = cum[0:1, :] if reverse else cum[q - 1:q, :]
    last_t = cum_t[:, 0:1] if reverse else cum_t[:, q - 1:q]
    wdt_t = jnp.exp(last_t - cum_t) * dt_t
    elast = jnp.exp(last)
    ecum = jnp.exp(cum)
    e1 = ecum.astype(BF16)
    r1 = ecum - e1.astype(F32)
    e2 = r1.astype(BF16)
    e3 = (r1 - e2.astype(F32)).astype(BF16)
    ecum_wide = jnp.dot(jnp.concatenate([e1, e2, e3], axis=1), sel_ref[...], preferred_element_type=F32)
    lo_q = lax.broadcasted_iota(jnp.int32, (q, V7X_LANES), 1) < SSD_HEAD_DIM
    lo_n = lax.broadcasted_iota(jnp.int32, (n, V7X_LANES), 1) < SSD_HEAD_DIM
    b_off = nheads * SSD_HEAD_DIM
    c_off = b_off + SSD_NGROUPS * n
    for g in range(SSD_NGROUPS):
        bg = xbc_ref[0, :, b_off + g * n:b_off + (g + 1) * n]
        bg_t = bg.T
        cb = xbc_ref[0, :, c_off + g * n:c_off + (g + 1) * n].astype(BF16)
        scores = lax.dot_general(cb, bg.astype(BF16), (((1,), (1,)), ((), ())), preferred_element_type=F32)
        for pr in range(SSD_HEADS_PER_GROUP // SSD_PAIR):
            pidx = g * (SSD_HEADS_PER_GROUP // SSD_PAIR) + pr
            lanes = slice(pidx * V7X_LANES, (pidx + 1) * V7X_LANES)
            xp = xbc_ref[0, :, lanes].astype(BF16)
            yd, stn = [], []
            for hh in range(SSD_PAIR):
                h = pidx * SSD_PAIR + hh
                seg = cum[:, h:h + 1] - cum_t[h:h + 1, :]
                dec = jnp.where(tri, jnp.exp(jnp.where(tri, seg, 0.0)), 0.0)
                w = (scores * dec * dt_t[h:h + 1, :]).astype(BF16)
                yd.append(jnp.dot(w, xp, preferred_element_type=F32))
                wb = (bg_t * wdt_t[h:h + 1, :]).astype(BF16)
                stn.append(jnp.dot(wb, xp, preferred_element_type=F32))
            h0 = pidx * SSD_PAIR
            s_prev = st_s[pidx]
            y_off = jnp.dot(cb, s_prev.astype(BF16), preferred_element_type=F32) * ecum_wide[:, lanes]
            st_s[pidx] = (s_prev * jnp.where(lo_n, elast[:, h0:h0 + 1], elast[:, h0 + 1:h0 + 2])
                          + jnp.where(lo_n, stn[0], stn[1]))
            yb_s[:, lanes] = jnp.where(lo_q, yd[0], yd[1]) + y_off
    sfin_ref[0] = st_s[...]
    if finish:
        xs = xbc_ref[0, :, :b_off]
        y = yf_ref[0] + yb_s[...] + dvec_ref[...] * xs
        t = y * _silu(z_ref[0])
        for g in range(SSD_NGROUPS):
            tg = t[:, g * gw:(g + 1) * gw]
            ms = jnp.mean(tg * tg, axis=-1, keepdims=True)
            y_ref[0, :, g * gw:(g + 1) * gw] = (tg * lax.rsqrt(ms + RMS_EPS)
                                                * nw_ref[:, g * gw:(g + 1) * gw]).astype(y_ref.dtype)


def ssd_scan(xbc, dtraw, dcol, dt_bias, a_head, s0, *, reverse, fin=None, q=256):
    b, l, cw = xbc.shape
    q = min(q, l)
    nc = l // q
    nheads = dt_bias.shape[0]
    width = nheads * SSD_HEAD_DIM
    npair = nheads // SSD_PAIR
    pad = lambda v: jnp.zeros((1, V7X_LANES), F32).at[0, :nheads].set(v.astype(F32))
    cmap = (lambda bb, c: (bb, nc - 1 - c, 0)) if reverse else (lambda bb, c: (bb, c, 0))
    dmap = (lambda bb, c: (bb, nc - 1 - c, dcol)) if reverse else (lambda bb, c: (bb, c, dcol))
    full = lambda shape: pl.BlockSpec(shape, lambda *_: (0,) * len(shape))
    st_spec = pl.BlockSpec((1, npair, SSD_STATE, V7X_LANES), lambda bb, c: (bb, 0, 0, 0))
    head_of_lane = np.arange(width) // SSD_HEAD_DIM
    sel = (np.arange(V7X_LANES)[:, None] == head_of_lane[None, :]).astype(np.float32)
    sel = jnp.asarray(np.concatenate([sel, sel, sel], axis=0), BF16)
    in_specs = [pl.BlockSpec((1, q, cw), cmap), pl.BlockSpec((1, q, V7X_LANES), dmap),
                full((1, V7X_LANES)), full((1, V7X_LANES)), full(sel.shape), st_spec]
    args = [xbc, dtraw, pad(dt_bias), pad(a_head), sel, s0]
    scratch = [pltpu.VMEM((npair, SSD_STATE, V7X_LANES), F32)]
    if fin is not None:
        y_other, z, zcol, dvec, nw = fin
        zmap = (lambda bb, c: (bb, nc - 1 - c, zcol)) if reverse else (lambda bb, c: (bb, c, zcol))
        in_specs += [pl.BlockSpec((1, q, width), cmap), pl.BlockSpec((1, q, width), zmap),
                     full((1, width)), full((1, width))]
        args += [y_other, z, dvec, nw]
        scratch.append(pltpu.VMEM((q, width), F32))
    out_dtype = BF16 if fin is not None else F32
    return pl.pallas_call(
        functools.partial(_ssd_kernel, reverse=reverse, finish=fin is not None, q=q,
                          gw=width // SSD_NGROUPS, nheads=nheads),
        grid=(b, nc),
        in_specs=in_specs,
        out_specs=[pl.BlockSpec((1, q, width), cmap), st_spec],
        out_shape=[jax.ShapeDtypeStruct((b, l, width), out_dtype),
                   jax.ShapeDtypeStruct((b, npair, SSD_STATE, V7X_LANES), F32)],
        scratch_shapes=scratch,
        compiler_params=_cparams(2, 4 * q * cw * 4 + 12 * q * width * 4 + 24 * q * q * 4),
        name="ssd_scan_bwd" if reverse else "ssd_scan_fwd",
    )(*args)


def ssd_mix(u, dtraw, zcol, xcol0, width, p, s0f, s0b):
    nheads = p["ssd_dt_bias"].shape[1]
    xbc = ssd_conv(u, xcol0, 2, width, p["ssd_conv_w"], p["ssd_conv_b"])
    a_head = -jnp.exp(p["ssd_a_log"].astype(F32))
    yf, sf = ssd_scan(xbc, dtraw, 0, p["ssd_dt_bias"][0], a_head[0], s0f, reverse=False)
    dvec = jnp.repeat(p["ssd_d"].astype(F32), SSD_HEAD_DIM).reshape(1, width)
    y, sb = ssd_scan(xbc, dtraw, 1, p["ssd_dt_bias"][1], a_head[1], s0b, reverse=True,
                     fin=(yf, u, zcol, dvec, p["ssd_norm_w"].reshape(1, width).astype(F32)))
    return y, sf, sb


def _cast_pad_kernel(x_ref, o_ref, *, nvalid):
    j = pl.program_id(1)

    @pl.when(j < nvalid)
    def _():
        o_ref[...] = x_ref[...].astype(o_ref.dtype)

    @pl.when(j >= nvalid)
    def _():
        o_ref[...] = jnp.zeros_like(o_ref)


def cast_pad_bf16(w, axis, tile, total):
    depth, r, c = w.shape
    size = w.shape[axis]
    assert size % tile == 0 and total % tile == 0
    nvalid = size // tile
    if axis == 2:
        block, out_shape = (1, r, tile), (depth, r, total)
        imap = lambda l, j: (l, 0, jnp.minimum(j, nvalid - 1))
        omap = lambda l, j: (l, 0, j)
    else:
        block, out_shape = (1, tile, c), (depth, total, c)
        imap = lambda l, j: (l, jnp.minimum(j, nvalid - 1), 0)
        omap = lambda l, j: (l, j, 0)
    nelem = block[1] * block[2]
    return pl.pallas_call(
        functools.partial(_cast_pad_kernel, nvalid=nvalid),
        grid=(depth, total // tile),
        in_specs=[pl.BlockSpec(block, imap)],
        out_specs=pl.BlockSpec(block, omap),
        out_shape=jax.ShapeDtypeStruct(out_shape, BF16),
        compiler_params=_cparams(2, 2 * nelem * 6),
        name="cast_pad_bf16",
    )(w)


def _dt_weights(w_in, n_main, nheads):
    pad_last = lambda w: jnp.pad(w, ((0, 0), (0, 0), (0, V7X_LANES - w.shape[2])))
    return jnp.concatenate([pad_last(w_in[:, :, n_main:n_main + nheads]),
                            pad_last(w_in[:, :, n_main + nheads:])], axis=2)


def _mixers(hm, wts, p, s5_h0, ssd_s0, want_output=True):
    b, l, d = hm.shape
    gw = d // N_MIXERS
    hm2 = hm.reshape(b * l, d)
    u = matmul(hm2, wts["w_in"], layer=wts["layer"], n_out=wts["n_main"], tn=1024,
               name="w_in_main").reshape(b, l, -1)
    dtraw = matmul(hm2, wts["w_dt"], layer=wts["layer"], tn=2 * V7X_LANES, name="w_in_dt").reshape(b, l, -1)
    ys, s5f, s5b = s5_mix(u, 1, gw, p, *s5_h0)
    yd, sdf, sdb = ssd_mix(u, dtraw, 4, 5, gw, p, *ssd_s0)
    states = ((s5f, s5b), (sdf, sdb))
    if not want_output:
        return None, states
    ya = fnet_mix(u, 0, gw)
    yc = conv_module(u, 2, 3, gw, p)
    return [y.reshape(b * l, gw) for y in (ya, ys, yc, yd)], states


def _layer_tail(h, mix, wts, mod, mod_next, ln, *, alpha, rows_per_batch):
    (ln1_g, ln1_b, ln2_g, ln2_b) = ln
    layer = wts["layer"]
    out = matmul(mix, wts["w_out"], layer=layer, tm=512, tn=1024, name="w_out")
    h, hm = residual_layernorm(h, out, mod[2], ln1_g, ln1_b, mod[4], mod[3], alpha=alpha,
                               rows_per_batch=rows_per_batch)
    hid = matmul(hm, [wts["w_gate"], wts["w_up"]], layer=layer, tm=1024, tn=FFN_TILE, out_dtype=BF16,
                 name="gate_up")
    ff = matmul(hid, wts["w_down"], layer=layer, tm=512, tn=512, name="w_down")
    return residual_layernorm(h, ff, mod[5], ln2_g, ln2_b, mod_next[1], mod_next[0], alpha=alpha,
                              rows_per_batch=rows_per_batch)


def kernel(x, c, ctx, c_ctx, w_ada, b_ada, w_in, s5_lam_re, s5_lam_im, s5_log_dt, s5_b_re, s5_b_im,
           s5_c_re, s5_c_im, s5_d, s5_w_glu, cv_w, cv_b, cv_ln_g, cv_ln_b, ssd_conv_w, ssd_conv_b,
           ssd_a_log, ssd_dt_bias, ssd_d, ssd_norm_w, w_out, ln1_g, ln1_b, w_gate, w_up, w_down,
           ln2_g, ln2_b):
    bsz, seq, d = x.shape
    cl = ctx.shape[1]
    depth = w_in.shape[0]
    gw = d // N_MIXERS
    alpha = (2 * depth) ** 0.25
    per_layer = dict(s5_lam_re=s5_lam_re, s5_lam_im=s5_lam_im, s5_log_dt=s5_log_dt,
                     s5_b_re=s5_b_re, s5_b_im=s5_b_im, s5_c_re=s5_c_re, s5_c_im=s5_c_im, s5_d=s5_d,
                     s5_w_glu=s5_w_glu, cv_w=cv_w, cv_b=cv_b, cv_ln_g=cv_ln_g, cv_ln_b=cv_ln_b,
                     ssd_conv_w=ssd_conv_w, ssd_conv_b=ssd_conv_b, ssd_a_log=ssd_a_log,
                     ssd_dt_bias=ssd_dt_bias, ssd_d=ssd_d, ssd_norm_w=ssd_norm_w)

    c8 = jnp.zeros((V7X_SUBLANES, d), F32).at[:bsz].set(c).at[bsz].set(c_ctx)
    mods = ada_modulation(c8, w_ada, b_ada)

    def mod_vectors(i):
        m = mods[i].reshape(V7X_SUBLANES, 6, d)
        lat = [m[:bsz, k].reshape(bsz, 1, d) for k in range(6)]
        cx = [jnp.broadcast_to(m[bsz, k].reshape(1, 1, d), (bsz, 1, d)) for k in range(6)]
        return lat, cx

    nstate = (gw // V7X_LANES) * S5_STATE_BLOCK
    npair = ssd_dt_bias.shape[2] // SSD_PAIR
    z_s5 = (jnp.zeros((bsz, 1, nstate), F32), jnp.zeros((bsz, 1, nstate), F32))
    z_ssd = jnp.zeros((bsz, npair, SSD_STATE, V7X_LANES), F32)

    n_main = 4 * gw + gw + (gw + 2 * SSD_NGROUPS * SSD_STATE)
    w_dt = _dt_weights(w_in, n_main, ssd_dt_bias.shape[2])
    hidden = w_gate.shape[2]
    hidden_pad = -(-hidden // FFN_TILE) * FFN_TILE
    cast_tile = math.gcd(hidden, FFN_TILE)
    w_gate_bf = cast_pad_bf16(w_gate, 2, cast_tile, hidden_pad)
    w_up_bf = cast_pad_bf16(w_up, 2, cast_tile, hidden_pad)
    w_down_bf = cast_pad_bf16(w_down, 1, cast_tile, hidden_pad)
    lat0, cx0 = mod_vectors(0)
    h, hm = prologue(x, lat0[1], lat0[0])
    h = h.reshape(bsz * seq, d)
    hc = ctx.reshape(bsz * cl, d)
    hcm = modulate(ctx, cx0[1], cx0[0])
    for i in range(depth):
        with_ctx = i < depth - 1
        p = {k: v[i] for k, v in per_layer.items()}
        wts = dict(layer=i, n_main=n_main, w_in=w_in, w_dt=w_dt, w_out=w_out, w_gate=w_gate_bf,
                   w_up=w_up_bf, w_down=w_down_bf)
        lat, cx = mod_vectors(i)
        lat_next, cx_next = mod_vectors(min(i + 1, depth - 1))
        ln = (ln1_g[i], ln1_b[i], ln2_g[i], ln2_b[i])
        mix_c, ((c5f, c5b), (cdf, cdb)) = _mixers(hcm.reshape(bsz, cl, d), wts, p, (z_s5, z_s5),
                                                   (z_ssd, z_ssd), want_output=with_ctx)
        mix_l, _ = _mixers(hm.reshape(bsz, seq, d), wts, p, (c5f, c5b), (cdf, cdb))
        h, hm = _layer_tail(h, mix_l, wts, lat, lat_next, ln, alpha=alpha, rows_per_batch=seq)
        if with_ctx:
            hc, hcm = _layer_tail(hc, mix_c, wts, cx, cx_next, ln, alpha=alpha, rows_per_batch=cl)
    return h.reshape(bsz, seq, d).astype(x.dtype)
```

```python
import functools
import math

import jax
import jax.numpy as jnp
import numpy as np
from jax import lax
from jax.experimental import pallas as pl
from jax.experimental.pallas import tpu as pltpu

F32 = jnp.float32
BF16 = jnp.bfloat16

GRID_W = 64
N_MIXERS = 4
FNET_HEADS = 4
S5_GROUP_CH = 16
S5_STATE = 64
CONV_WIDTH = 31
SSD_HEAD_DIM = 64
SSD_NGROUPS = 4
SSD_STATE = 128
SSD_CONV = 5
LN_EPS = 1e-5
RMS_EPS = 1e-5

V7X_LANES = 128
V7X_SUBLANES = 8
V7X_VMEM_LIMIT_BYTES = 60000 * 1024
FFN_TILE = 512


def _cparams(n_axes, vmem_bytes):
    limit = int(min(max(vmem_bytes * 5 // 4 + (2 << 20), 16 << 20), V7X_VMEM_LIMIT_BYTES))
    return pltpu.CompilerParams(dimension_semantics=("arbitrary",) * n_axes,
                                vmem_limit_bytes=limit)


def _silu(x):
    return x * jax.nn.sigmoid(x)


def _gelu_tanh(x):
    return 0.5 * x * (1.0 + jnp.tanh(math.sqrt(2.0 / math.pi) * (x + 0.044715 * (x * x * x))))


def _softplus(x):
    return jnp.maximum(x, 0.0) + jnp.log1p(jnp.exp(-jnp.abs(x)))


def _ada_kernel(c_ref, w_ref, b_ref, o_ref):
    cs = _silu(c_ref[...])
    ch = cs.astype(BF16)
    cl = (cs - ch.astype(F32)).astype(BF16)
    w = w_ref[0]
    wh = w.astype(BF16)
    wl = (w - wh.astype(F32)).astype(BF16)
    rows = cs.shape[0]
    top = jnp.dot(jnp.concatenate([ch, cl], axis=0), wh, preferred_element_type=F32)
    o_ref[0] = top[:rows] + top[rows:] + jnp.dot(ch, wl, preferred_element_type=F32) + b_ref[0]


def ada_modulation(c8, w_ada, b_ada, tn=1024):
    depth, d, n = w_ada.shape
    tn = min(tn, n)
    return pl.pallas_call(
        _ada_kernel,
        grid=(depth, n // tn),
        in_specs=[pl.BlockSpec((V7X_SUBLANES, d), lambda l, j: (0, 0)),
                  pl.BlockSpec((1, d, tn), lambda l, j: (l, 0, j)),
                  pl.BlockSpec((1, 1, tn), lambda l, j: (l, 0, j))],
        out_specs=pl.BlockSpec((1, V7X_SUBLANES, tn), lambda l, j: (l, 0, j)),
        out_shape=jax.ShapeDtypeStruct((depth, V7X_SUBLANES, n), F32),
        compiler_params=_cparams(2, 2 * d * tn * 4 + 6 * d * tn * 2),
        name="ada_modulation",
    )(c8, w_ada, b_ada.reshape(depth, 1, n))


def _prologue_kernel(x_ref, tr_ref, tc_ref, sc_ref, sh_ref, h_ref, hm_ref, *, tp, half):
    sc = 1.0 + sc_ref[0]
    sh = sh_ref[0]
    for rr in range(tp // GRID_W):
        rows = slice(rr * GRID_W, (rr + 1) * GRID_W)
        lo = x_ref[0, rows, :half] + tr_ref[rr:rr + 1, :]
        hi = x_ref[0, rows, half:] + tc_ref[...]
        h_ref[0, rows, :half] = lo
        h_ref[0, rows, half:] = hi
        hm_ref[0, rows, :half] = (lo * sc[:, :half] + sh[:, :half]).astype(BF16)
        hm_ref[0, rows, half:] = (hi * sc[:, half:] + sh[:, half:]).astype(BF16)


def prologue(x, sc, sh, tp=512):
    b, l, d = x.shape
    tp = min(tp, l)
    half = d // 2
    q = d // 4
    omega = jnp.exp(-math.log(10000.0) * jnp.arange(q, dtype=F32) / q)[None]
    r = jnp.arange(l // GRID_W, dtype=F32)[:, None]
    col = jnp.arange(GRID_W, dtype=F32)[:, None]
    tab_r = jnp.concatenate([jnp.sin(r * omega), jnp.cos(r * omega)], -1)
    tab_c = jnp.concatenate([jnp.sin(col * omega), jnp.cos(col * omega)], -1)
    rpt = tp // GRID_W
    return pl.pallas_call(
        functools.partial(_prologue_kernel, tp=tp, half=half),
        grid=(b, l // tp),
        in_specs=[pl.BlockSpec((1, tp, d), lambda bb, i: (bb, i, 0)),
                  pl.BlockSpec((rpt, half), lambda bb, i: (i, 0)),
                  pl.BlockSpec((GRID_W, half), lambda bb, i: (0, 0)),
                  pl.BlockSpec((1, 1, d), lambda bb, i: (bb, 0, 0)),
                  pl.BlockSpec((1, 1, d), lambda bb, i: (bb, 0, 0))],
        out_specs=[pl.BlockSpec((1, tp, d), lambda bb, i: (bb, i, 0)),
                   pl.BlockSpec((1, tp, d), lambda bb, i: (bb, i, 0))],
        out_shape=[jax.ShapeDtypeStruct((b, l, d), F32), jax.ShapeDtypeStruct((b, l, d), BF16)],
        compiler_params=_cparams(2, 2 * tp * d * (4 + 4 + 2)),
        name="prologue",
    )(x, tab_r, tab_c, sc, sh)


def _modulate_kernel(x_ref, sc_ref, sh_ref, o_ref):
    o_ref[0] = (x_ref[0] * (1.0 + sc_ref[0]) + sh_ref[0]).astype(BF16)


def modulate(x, sc, sh, tp=256):
    b, l, d = x.shape
    tp = min(tp, l)
    return pl.pallas_call(
        _modulate_kernel,
        grid=(b, l // tp),
        in_specs=[pl.BlockSpec((1, tp, d), lambda bb, i: (bb, i, 0)),
                  pl.BlockSpec((1, 1, d), lambda bb, i: (bb, 0, 0)),
                  pl.BlockSpec((1, 1, d), lambda bb, i: (bb, 0, 0))],
        out_specs=pl.BlockSpec((1, tp, d), lambda bb, i: (bb, i, 0)),
        out_shape=jax.ShapeDtypeStruct((b, l, d), BF16),
        compiler_params=_cparams(2, 2 * tp * d * 6),
        name="modulate",
    )(x, sc, sh)


def _mm_kernel(*refs, n_lhs, n_w, cast_w):
    x_refs = refs[:n_lhs]
    w_refs = refs[n_lhs:n_lhs + n_w]
    o_ref = refs[n_lhs + n_w]
    if cast_w:
        wb = refs[n_lhs + n_w + 1:]

        @pl.when(pl.program_id(1) == 0)
        def _():
            for w_ref, s in zip(w_refs, wb):
                s[...] = w_ref[...].astype(BF16)

        ws = [s[...] for s in wb]
    else:
        ws = [w_ref[...] for w_ref in w_refs]
    xs = [x_ref[...].astype(BF16) for x_ref in x_refs]
    x = xs[0] if n_lhs == 1 else jnp.concatenate(xs, axis=1)
    acc = [jnp.dot(x, w, preferred_element_type=F32) for w in ws]
    y = acc[0] if n_w == 1 else _silu(acc[0]) * acc[1]
    o_ref[...] = y.astype(o_ref.dtype)


def matmul(xs, ws, *, layer, n_out=None, tm=1024, tn=512, out_dtype=F32, name="matmul"):
    slabs = list(xs) if isinstance(xs, (list, tuple)) else [xs]
    wl = list(ws) if isinstance(ws, (list, tuple)) else [ws]
    m = slabs[0].shape[0]
    kk, n = wl[0].shape[-2:]
    n = n if n_out is None else n_out
    tm, tn = min(tm, m), min(tn, n)
    assert m % tm == 0 and n % tn == 0
    cast_w = wl[0].dtype != BF16
    x_specs = [pl.BlockSpec((tm, s.shape[1]), lambda j, i: (i, 0)) for s in slabs]
    if cast_w:
        mode = dict(pipeline_mode=pl.Buffered(1)) if m > tm else {}
        w_specs = [pl.BlockSpec((None, kk, tn), lambda j, i: (layer, 0, j), **mode) for _ in wl]
        scratch = [pltpu.VMEM((kk, tn), BF16) for _ in wl]
    else:
        w_specs = [pl.BlockSpec((None, kk, tn), lambda j, i: (layer, 0, j)) for _ in wl]
        scratch = []
    ob = jnp.dtype(out_dtype).itemsize
    xbytes = sum(tm * s.shape[1] * s.dtype.itemsize for s in slabs)
    wbytes = len(wl) * kk * tn * (((4 if m > tm else 2 * 4) + 2) if cast_w else 2 * 2)
    return pl.pallas_call(
        functools.partial(_mm_kernel, n_lhs=len(slabs), n_w=len(wl), cast_w=cast_w),
        grid=(n // tn, m // tm),
        in_specs=x_specs + w_specs,
        out_specs=pl.BlockSpec((tm, tn), lambda j, i: (i, j)),
        out_shape=jax.ShapeDtypeStruct((m, n), out_dtype),
        scratch_shapes=scratch,
        compiler_params=_cparams(2, 2 * xbytes + wbytes + 2 * tm * tn * ob + (len(wl) + 1) * tm * tn * 4
                                 + (tm * kk * 2 if len(slabs) > 1 else 0)),
        name=name,
    )(*slabs, *wl)


def _ln_kernel(h_ref, mix_ref, g_ref, lng_ref, lnb_ref, sc_ref, sh_ref, ho_ref, hm_ref, *, alpha):
    r = alpha * h_ref[...] + g_ref[0] * mix_ref[...]
    mu = jnp.mean(r, axis=-1, keepdims=True)
    rc = r - mu
    var = jnp.mean(rc * rc, axis=-1, keepdims=True)
    y = rc * lax.rsqrt(var + LN_EPS) * lng_ref[...] + lnb_ref[...]
    ho_ref[...] = y
    hm_ref[...] = (y * (1.0 + sc_ref[0]) + sh_ref[0]).astype(BF16)


def residual_layernorm(h, mix, gate, ln_g, ln_b, sc, sh, *, alpha, rows_per_batch, tm=256):
    m, n = h.shape
    tm = min(tm, m, rows_per_batch)
    row = pl.BlockSpec((tm, n), lambda i: (i, 0))
    vec_b = pl.BlockSpec((1, 1, n), lambda i: ((i * tm) // rows_per_batch, 0, 0))
    vec = pl.BlockSpec((1, n), lambda i: (0, 0))
    return pl.pallas_call(
        functools.partial(_ln_kernel, alpha=alpha),
        grid=(m // tm,),
        in_specs=[row, row, vec_b, vec, vec, vec_b, vec_b],
        out_specs=[row, row],
        out_shape=[jax.ShapeDtypeStruct((m, n), F32), jax.ShapeDtypeStruct((m, n), BF16)],
        compiler_params=_cparams(1, 2 * tm * n * (4 + 4 + 4 + 2) + 4 * tm * n * 4),
        name="residual_layernorm",
    )(h, mix, gate, ln_g.reshape(1, n), ln_b.reshape(1, n), sc, sh)


FNET_N2 = 128


def _dft_cos_sin(n):
    k = np.arange(n, dtype=np.int64)
    ang = (2.0 * np.pi / n) * ((k[:, None] * k[None, :]) % n).astype(np.float64)
    return np.cos(ang), np.sin(ang)


def _split_bf16(m):
    m = jnp.asarray(m, F32)
    hi = m.astype(BF16)
    return hi, (m - hi.astype(F32)).astype(BF16)


def _const_lhs3(m):
    hi, lo = _split_bf16(m)
    return jnp.concatenate([hi, hi, lo], axis=1)


def _const_rhs3(m):
    hi, lo = _split_bf16(m)
    return jnp.concatenate([hi, hi, lo], axis=0)


def _dot3_const_lhs(m3, x):
    xh = x.astype(BF16)
    xl = (x - xh.astype(F32)).astype(BF16)
    return jnp.dot(m3, jnp.concatenate([xh, xl, xh], axis=0), preferred_element_type=F32)


def _dot3_const_rhs(x, r3):
    xh = x.astype(BF16)
    xl = (x - xh.astype(F32)).astype(BF16)
    return jnp.dot(jnp.concatenate([xh, xl, xh], axis=1), r3, preferred_element_type=F32)


def _channel_dft(x, cs3_ref, hd):
    ps, qs = [], []
    for hh in range(x.shape[1] // hd):
        pq = _dot3_const_rhs(x[:, hh * hd:(hh + 1) * hd], cs3_ref[...])
        ps.append(pq[:, :hd])
        qs.append(pq[:, hd:])
    return jnp.concatenate(ps, axis=1), jnp.concatenate(qs, axis=1)


def _fnet_a_kernel(x_ref, cs3_ref, m1_ref, twr_ref, twi_ref, ar_ref, ai_ref, *, hd, n1):
    nsub = x_ref.shape[2]
    x = jnp.concatenate([x_ref[0, :, j, :] for j in range(nsub)], axis=0)
    p, q = _channel_dft(x, cs3_ref, hd)
    reps = x.shape[1] // V7X_LANES
    for j in range(nsub):
        rows = slice(j * n1, (j + 1) * n1)
        mp = _dot3_const_lhs(m1_ref[...], p[rows])
        mq = _dot3_const_lhs(m1_ref[...], q[rows])
        ar = mp[:n1] - mq[n1:]
        ai = -(mq[:n1] + mp[n1:])
        tr = jnp.tile(twr_ref[j], (1, reps))
        ti = jnp.tile(twi_ref[j], (1, reps))
        ar_ref[0, j] = ar * tr - ai * ti
        ai_ref[0, j] = ar * ti + ai * tr


def _fnet_c_kernel(ar_ref, ai_ref, m2_ref, o_ref, *, scale):
    for j in range(ar_ref.shape[2]):
        z = jnp.concatenate([ar_ref[0, :, j, :], ai_ref[0, :, j, :]], axis=0)
        o_ref[0, :, j, :] = _dot3_const_lhs(m2_ref[...], z) * scale


def _fnet_dense_kernel(x_ref, cs3_ref, ml_ref, o_ref, *, hd, scale):
    p, q = _channel_dft(x_ref[0], cs3_ref, hd)
    o_ref[0] = _dot3_const_lhs(ml_ref[...], jnp.concatenate([p, q], axis=0)) * scale


def fnet_mix(u, colblk, width):
    b, l, ctot = u.shape
    hd = width // FNET_HEADS
    scale = 1.0 / math.sqrt(l * hd)
    cd, sd = _dft_cos_sin(hd)
    cs3 = _const_rhs3(np.concatenate([cd, sd], axis=1))
    full = lambda shape: pl.BlockSpec(shape, lambda *_: (0,) * len(shape))
    n2 = FNET_N2
    if l < V7X_SUBLANES * n2:
        cl, sl = _dft_cos_sin(l)
        ml = _const_lhs3(np.concatenate([cl, -sl], axis=1))
        return pl.pallas_call(
            functools.partial(_fnet_dense_kernel, hd=hd, scale=scale),
            grid=(b,),
            in_specs=[pl.BlockSpec((1, l, width), lambda bb: (bb, 0, colblk)), full(cs3.shape), full(ml.shape)],
            out_specs=pl.BlockSpec((1, l, width), lambda bb: (bb, 0, 0)),
            out_shape=jax.ShapeDtypeStruct((b, l, width), F32),
            compiler_params=_cparams(1, 16 * l * width * 4),
            name="fnet_dense",
        )(u, cs3, ml)
    n1 = l // n2
    nsub = V7X_SUBLANES
    c1, s1 = _dft_cos_sin(n1)
    c2, s2 = _dft_cos_sin(n2)
    m1 = _const_lhs3(np.concatenate([c1, s1], axis=0))
    m2 = _const_lhs3(np.concatenate([c2, s2], axis=1))
    kk = (np.arange(n2, dtype=np.int64)[:, None] * np.arange(n1, dtype=np.int64)[None, :]) % l
    ang = (2.0 * np.pi / l) * kk.astype(np.float64)
    twr = jnp.asarray(np.broadcast_to(np.cos(ang)[:, :, None], (n2, n1, V7X_LANES)), F32)
    twi = jnp.asarray(np.broadcast_to(-np.sin(ang)[:, :, None], (n2, n1, V7X_LANES)), F32)
    ar, ai = pl.pallas_call(
        functools.partial(_fnet_a_kernel, hd=hd, n1=n1),
        grid=(b, n2 // nsub),
        in_specs=[pl.BlockSpec((1, n1, nsub, width), lambda bb, j: (bb, 0, j, colblk)),
                  full(cs3.shape), full(m1.shape),
                  pl.BlockSpec((nsub, n1, V7X_LANES), lambda bb, j: (j, 0, 0)),
                  pl.BlockSpec((nsub, n1, V7X_LANES), lambda bb, j: (j, 0, 0))],
        out_specs=[pl.BlockSpec((1, nsub, n1, width), lambda bb, j: (bb, j, 0, 0))] * 2,
        out_shape=[jax.ShapeDtypeStruct((b, n2, n1, width), F32)] * 2,
        compiler_params=_cparams(2, 40 * nsub * n1 * width * 4),
        name="fnet_stage_a",
    )(u.reshape(b, n1, n2, ctot), cs3, m1, twr, twi)
    y = pl.pallas_call(
        functools.partial(_fnet_c_kernel, scale=scale),
        grid=(b, n1 // nsub),
        in_specs=[pl.BlockSpec((1, n2, nsub, width), lambda bb, j: (bb, 0, j, 0)),
                  pl.BlockSpec((1, n2, nsub, width), lambda bb, j: (bb, 0, j, 0)),
                  full(m2.shape)],
        out_specs=pl.BlockSpec((1, n2, nsub, width), lambda bb, j: (bb, 0, j, 0)),
        out_shape=jax.ShapeDtypeStruct((b, n2, n1, width), F32),
        compiler_params=_cparams(2, 10 * n2 * nsub * width * 4),
        name="fnet_stage_c",
    )(ar, ai, m2)
    return y.reshape(b, l, width)


S5_LANE_BLOCK_GROUPS = V7X_LANES // S5_GROUP_CH
S5_STATE_BLOCK = S5_LANE_BLOCK_GROUPS * S5_STATE
S5_LEVELS = (1, 2, 4)
S5_SEGMENTS = V7X_SUBLANES


def _s5_direction_params(p, k, reverse, tseg):
    g, st = p["s5_lam_re"].shape[1:]
    lr, li = p["s5_lam_re"][k].astype(F32), p["s5_lam_im"][k].astype(F32)
    dt = jnp.exp(p["s5_log_dt"][k].astype(F32))[:, None]
    zr, zi = lr * dt, li * dt

    def power(s):
        e = jnp.exp(s * zr)
        return e * jnp.cos(s * zi), e * jnp.sin(s * zi)

    ar, ai = power(1.0)
    nr, ni = ar - 1.0, ai
    den = lr * lr + li * li
    cr, ci = (nr * lr + ni * li) / den, (ni * lr - nr * li) / den
    b_re, b_im = p["s5_b_re"].astype(F32), p["s5_b_im"].astype(F32)
    btr = b_re * cr[..., None] - b_im * ci[..., None]
    bti = b_re * ci[..., None] + b_im * cr[..., None]
    nj = g // S5_LANE_BLOCK_GROUPS
    eye = jnp.eye(S5_LANE_BLOCK_GROUPS, dtype=F32)

    def blockdiag_b(bt):
        b4 = bt.reshape(nj, S5_LANE_BLOCK_GROUPS, st, S5_GROUP_CH)
        return jnp.einsum("jgpc,gh->jgchp", b4, eye).reshape(nj, V7X_LANES, S5_STATE_BLOCK).astype(BF16)

    t = jnp.arange(V7X_SUBLANES)
    tabs_r, tabs_i = [], []
    for s in S5_LEVELS:
        pr, pi = power(float(s * tseg))
        keep = (t <= V7X_SUBLANES - 1 - s) if reverse else (t >= s)
        tabs_r.append(jnp.where(keep[:, None], pr.reshape(1, -1), 0.0))
        tabs_i.append(jnp.where(keep[:, None], pi.reshape(1, -1), 0.0))
    steps = ((V7X_SUBLANES - t) if reverse else (t + 1))[:, None, None].astype(F32) * float(tseg)
    e = jnp.exp(steps * zr[None])
    tabs_r.append((e * jnp.cos(steps * zi[None])).reshape(V7X_SUBLANES, -1))
    tabs_i.append((e * jnp.sin(steps * zi[None])).reshape(V7X_SUBLANES, -1))
    tabs_r.append(jnp.broadcast_to(ar.reshape(1, -1), (V7X_SUBLANES, ar.size)))
    tabs_i.append(jnp.broadcast_to(ai.reshape(1, -1), (V7X_SUBLANES, ai.size)))
    return blockdiag_b(btr), blockdiag_b(bti), jnp.stack(tabs_r), jnp.stack(tabs_i)


def _s5_output_params(p):
    g, ch, st = p["s5_c_re"].shape
    nj = g // S5_LANE_BLOCK_GROUPS
    eye = jnp.eye(S5_LANE_BLOCK_GROUPS, dtype=F32)

    def blockdiag_c(c):
        c4 = c.astype(F32).reshape(nj, S5_LANE_BLOCK_GROUPS, ch, st)
        return jnp.einsum("jgcp,gh->jgphc", c4, eye).reshape(nj, S5_STATE_BLOCK, V7X_LANES).astype(BF16)

    return blockdiag_c(p["s5_c_re"]), blockdiag_c(p["s5_c_im"])


def _s5_kernel(*refs, reverse, finish, tseg, nj):
    if finish:
        (u_ref, h0r_ref, h0i_ref, bre_ref, bim_ref, cre_ref, cim_ref, tr_ref, ti_ref,
         yf_ref, d_ref, wglu_ref, y_ref, hfr_ref, hfi_ref,
         up_s, xr_s, xi_s, hr_s, hi_s, car_s, cai_s, yp_s, yt_s) = refs
    else:
        (u_ref, h0r_ref, h0i_ref, bre_ref, bim_ref, cre_ref, cim_ref, tr_ref, ti_ref,
         y_ref, hfr_ref, hfi_ref, up_s, xr_s, xi_s, hr_s, hi_s, car_s, cai_s) = refs

    @pl.when(pl.program_id(1) == 0)
    def _():
        car_s[...] = h0r_ref[0]
        cai_s[...] = h0i_ref[0]

    nseg = S5_SEGMENTS
    sb = S5_STATE_BLOCK
    tile = lambda t: pl.ds(pl.multiple_of(t * nseg, nseg), nseg)

    def gather(t, c):
        up_s[tile(t), :] = u_ref[0, :, t, :]
        return c

    lax.fori_loop(0, tseg, gather, 0)
    edge = 0 if reverse else nseg - 1
    first = nseg - 1 if reverse else 0
    for j in range(nj):
        lanes = slice(j * V7X_LANES, (j + 1) * V7X_LANES)
        states = slice(j * sb, (j + 1) * sb)
        ub = up_s[:, lanes].astype(BF16)
        xr_s[...] = jnp.dot(ub, bre_ref[j], preferred_element_type=F32)
        xi_s[...] = jnp.dot(ub, bim_ref[j], preferred_element_type=F32)
        ar = tr_ref[len(S5_LEVELS) + 1, :, states]
        ai = ti_ref[len(S5_LEVELS) + 1, :, states]

        def step(t, carry, store, ar=ar, ai=ai):
            hr, hi = carry
            rows = tile((tseg - 1 - t) if reverse else t)
            hr, hi = ar * hr - ai * hi + xr_s[rows, :], ar * hi + ai * hr + xi_s[rows, :]
            if store:
                hr_s[rows, :] = hr
                hi_s[rows, :] = hi
            return hr, hi

        zero = jnp.zeros((nseg, sb), F32)
        gr, gi = lax.fori_loop(0, tseg, functools.partial(step, store=False), (zero, zero), unroll=4)
        for lvl, s in enumerate(S5_LEVELS):
            pr = tr_ref[lvl, :, states]
            pi = ti_ref[lvl, :, states]
            shift = (nseg - s) if reverse else s
            sr = pltpu.roll(gr, shift, 0)
            si = pltpu.roll(gi, shift, 0)
            gr, gi = gr + pr * sr - pi * si, gi + pr * si + pi * sr
        pr = tr_ref[len(S5_LEVELS), :, states]
        pi = ti_ref[len(S5_LEVELS), :, states]
        cr = jnp.broadcast_to(car_s[:, states], (nseg, sb))
        ci = jnp.broadcast_to(cai_s[:, states], (nseg, sb))
        gr, gi = gr + pr * cr - pi * ci, gi + pr * ci + pi * cr
        car_s[:, states] = gr[edge:edge + 1, :]
        cai_s[:, states] = gi[edge:edge + 1, :]
        row = lax.broadcasted_iota(jnp.int32, (nseg, sb), 0)
        shift = (nseg - 1) if reverse else 1
        start = (jnp.where(row == first, cr, pltpu.roll(gr, shift, 0)),
                 jnp.where(row == first, ci, pltpu.roll(gi, shift, 0)))
        lax.fori_loop(0, tseg, functools.partial(step, store=True), start, unroll=4)
        yp = (jnp.dot(hr_s[...].astype(BF16), cre_ref[j], preferred_element_type=F32)
              - jnp.dot(hi_s[...].astype(BF16), cim_ref[j], preferred_element_type=F32))
        if finish:
            yp_s[:, lanes] = yp + yf_ref[0, :, lanes]
        else:
            y_ref[0, :, lanes] = yp

    hfr_ref[0] = car_s[...]
    hfi_ref[0] = cai_s[...]
    if finish:
        def scatter(t, c):
            yt_s[:, t, :] = yp_s[tile(t), :]
            return c

        lax.fori_loop(0, tseg, scatter, 0)
        for s in range(nseg):
            y = yt_s[s] + d_ref[...] * u_ref[0, s]
            gl = _gelu_tanh(y)
            gate = jnp.dot(gl.astype(BF16), wglu_ref[...], preferred_element_type=F32)
            y_ref[0, s * tseg:(s + 1) * tseg, :] = (gl * jax.nn.sigmoid(gate)).astype(y_ref.dtype)


def s5_scan(u, colblk, width, dirp, outp, h0, *, reverse, tseg, fin=None):
    b, l, ctot = u.shape
    nseg = S5_SEGMENTS
    tt = nseg * tseg
    nc = l // tt
    nj = width // V7X_LANES
    ns = nj * S5_STATE_BLOCK
    bre, bim, tr, ti = dirp
    cre, cim = outp
    h0r, h0i = h0
    cmap = (lambda bb, c: (bb, nc - 1 - c, 0, colblk)) if reverse else (lambda bb, c: (bb, c, 0, colblk))
    omap = (lambda bb, c: (bb, nc - 1 - c, 0)) if reverse else (lambda bb, c: (bb, c, 0))
    full = lambda shape: pl.BlockSpec(shape, lambda *_: (0,) * len(shape))
    st_spec = pl.BlockSpec((1, 1, ns), lambda bb, c: (bb, 0, 0))
    in_specs = [pl.BlockSpec((1, nseg, tseg, width), cmap), st_spec, st_spec,
                full(bre.shape), full(bim.shape), full(cre.shape), full(cim.shape),
                full(tr.shape), full(ti.shape)]
    args = [u.reshape(b, l // tseg, tseg, ctot), h0r, h0i, bre, bim, cre, cim, tr, ti]
    state_tile = pltpu.VMEM((tt, S5_STATE_BLOCK), F32)
    scratch = [pltpu.VMEM((tt, width), F32), state_tile, state_tile, state_tile, state_tile,
               pltpu.VMEM((1, ns), F32), pltpu.VMEM((1, ns), F32)]
    vmem = 3 * tt * width * 4 + 4 * tt * S5_STATE_BLOCK * 4 + 2 * tt * width * 4
    if fin is not None:
        y_other, d, wglu = fin
        in_specs += [pl.BlockSpec((1, tt, width), omap), full((1, width)), full(wglu.shape)]
        args += [y_other, d.reshape(1, width).astype(F32), wglu]
        scratch += [pltpu.VMEM((tt, width), F32), pltpu.VMEM((nseg, tseg, width), F32)]
        vmem += 4 * tt * width * 4 + 2 * wglu.size * 2
    out_dtype = BF16 if fin is not None else F32
    vmem += 2 * 4 * bre.size * 2 + 2 * 2 * tr.size * 4
    return pl.pallas_call(
        functools.partial(_s5_kernel, reverse=reverse, finish=fin is not None, tseg=tseg, nj=nj),
        grid=(b, nc),
        in_specs=in_specs,
        out_specs=[pl.BlockSpec((1, tt, width), omap), st_spec, st_spec],
        out_shape=[jax.ShapeDtypeStruct((b, l, width), out_dtype),
                   jax.ShapeDtypeStruct((b, 1, ns), F32), jax.ShapeDtypeStruct((b, 1, ns), F32)],
        scratch_shapes=scratch,
        compiler_params=_cparams(2, vmem),
        name="s5_scan_bwd" if reverse else "s5_scan_fwd",
    )(*args)


def s5_mix(u, colblk, width, p, h0f, h0b, tseg=128):
    tseg = min(tseg, u.shape[1] // S5_SEGMENTS)
    outp = _s5_output_params(p)
    yf, hf_r, hf_i = s5_scan(u, colblk, width, _s5_direction_params(p, 0, False, tseg), outp, h0f,
                             reverse=False, tseg=tseg)
    y, hb_r, hb_i = s5_scan(u, colblk, width, _s5_direction_params(p, 1, True, tseg), outp, h0b,
                            reverse=True, tseg=tseg, fin=(yf, p["s5_d"], p["s5_w_glu"].astype(BF16)))
    return y, (hf_r, hf_i), (hb_r, hb_i)


def _fill_padded(pad_s, main, prev, nxt, halo):
    i = pl.program_id(1)
    n = pl.num_programs(1)
    tt = main.shape[0]
    pad_s[0:halo, :] = jnp.where(i > 0, prev, 0.0)
    pad_s[halo:halo + tt, :] = main
    pad_s[halo + tt:halo + tt + halo, :] = jnp.where(i < n - 1, nxt, 0.0)


def _dwconv_rows(pad_s, w_ref, width, halo, tt, row_block=32):
    off = halo - (width - 1) // 2
    outs = []
    for r0 in range(0, tt, row_block):
        rb = min(row_block, tt - r0)
        acc = None
        for k in range(width):
            term = w_ref[k:k + 1, :] * pad_s[r0 + off + k:r0 + off + k + rb, :]
            acc = term if acc is None else acc + term
        outs.append(acc)
    return outs


def _halo_specs(tt, halo, width, colblk, l):
    r = tt // halo
    nh = l // halo
    main = pl.BlockSpec((1, tt, width), lambda bb, i: (bb, i, colblk))
    prev = pl.BlockSpec((1, halo, width), lambda bb, i: (bb, jnp.maximum(i * r - 1, 0), colblk))
    nxt = pl.BlockSpec((1, halo, width), lambda bb, i: (bb, jnp.minimum((i + 1) * r, nh - 1), colblk))
    return [main, prev, nxt]


CV_HALO = 16


CV_ROW_BLOCK = 64


def _conv_module_kernel(v_ref, vp_ref, vn_ref, g_ref, gp_ref, gn_ref, w_ref, b_ref, lg_ref, lb_ref,
                        o_ref, pad_s, acc_s, *, tt):
    glu = lambda v, g: v * jax.nn.sigmoid(g)
    _fill_padded(pad_s, glu(v_ref[0], g_ref[0]), glu(vp_ref[0], gp_ref[0]), glu(vn_ref[0], gn_ref[0]), CV_HALO)
    sub, rb = V7X_SUBLANES, CV_ROW_BLOCK
    off = CV_HALO - (CONV_WIDTH - 1) // 2

    def row_block(bi, c):
        r0 = pl.multiple_of(bi * rb, rb)
        for lb in range(acc_s.shape[1] // V7X_LANES):
            lanes = slice(lb * V7X_LANES, (lb + 1) * V7X_LANES)
            y = None
            for r in range(sub):
                part = None
                for m in range(off, off + CONV_WIDTH):
                    if m % sub != r:
                        continue
                    rows = pl.ds(pl.multiple_of(r0 + (m // sub) * sub, sub), rb + sub)
                    term = w_ref[m - off:m - off + 1, lanes] * pad_s[rows, lanes]
                    part = term if part is None else part + term
                if part is not None:
                    y = part[r:r + rb, :] if y is None else y + part[r:r + rb, :]
            acc_s[pl.ds(r0, rb), lanes] = y
        hcv = acc_s[pl.ds(r0, rb), :] + b_ref[...]
        mu = jnp.mean(hcv, axis=-1, keepdims=True)
        hc = hcv - mu
        var = jnp.mean(hc * hc, axis=-1, keepdims=True)
        yn = hc * lax.rsqrt(var + LN_EPS) * lg_ref[...] + lb_ref[...]
        o_ref[0, pl.ds(r0, rb), :] = _silu(yn).astype(o_ref.dtype)
        return c

    lax.fori_loop(0, tt // rb, row_block, 0)


def conv_module(u, colblk_v, colblk_g, width, p, tt=256):
    b, l, _ = u.shape
    tt = min(tt, l)
    assert tt % CV_ROW_BLOCK == 0
    wk = jnp.zeros((32, width), F32).at[:CONV_WIDTH].set(p["cv_w"].astype(F32))
    vec = pl.BlockSpec((1, width), lambda bb, i: (0, 0))
    return pl.pallas_call(
        functools.partial(_conv_module_kernel, tt=tt),
        grid=(b, l // tt),
        in_specs=_halo_specs(tt, CV_HALO, width, colblk_v, l) + _halo_specs(tt, CV_HALO, width, colblk_g, l)
        + [pl.BlockSpec((32, width), lambda bb, i: (0, 0)), vec, vec, vec],
        out_specs=pl.BlockSpec((1, tt, width), lambda bb, i: (bb, i, 0)),
        out_shape=jax.ShapeDtypeStruct((b, l, width), BF16),
        scratch_shapes=[pltpu.VMEM((tt + 2 * CV_HALO, width), F32), pltpu.VMEM((tt, width), F32)],
        compiler_params=_cparams(2, 12 * (tt + 2 * CV_HALO) * width * 4),
        name="conv_module",
    )(u, u, u, u, u, u, wk, p["cv_b"].reshape(1, width).astype(F32),
      p["cv_ln_g"].reshape(1, width).astype(F32), p["cv_ln_b"].reshape(1, width).astype(F32))


SSD_HALO = 8
SSD_HEADS_PER_GROUP = 4
SSD_PAIR = V7X_LANES // SSD_HEAD_DIM


def _ssd_conv_kernel(x_ref, xp_ref, xn_ref, w_ref, b_ref, o_ref, pad_s, *, tt):
    _fill_padded(pad_s, x_ref[0], xp_ref[0], xn_ref[0], SSD_HALO)
    row_block = 64
    for bi, acc in enumerate(_dwconv_rows(pad_s, w_ref, SSD_CONV, SSD_HALO, tt, row_block)):
        o_ref[0, bi * row_block:bi * row_block + acc.shape[0], :] = _silu(acc + b_ref[...])


def ssd_conv(u, colblk0, ncol, width, conv_w, conv_b, tt=1024):
    b, l, _ = u.shape
    tt = min(tt, l)
    r = tt // SSD_HALO
    nh = l // SSD_HALO
    wk = jnp.zeros((V7X_SUBLANES, ncol * width), F32).at[:SSD_CONV].set(conv_w.astype(F32))
    return pl.pallas_call(
        functools.partial(_ssd_conv_kernel, tt=tt),
        grid=(b, l // tt, ncol),
        in_specs=[pl.BlockSpec((1, tt, width), lambda bb, i, c: (bb, i, colblk0 + c)),
                  pl.BlockSpec((1, SSD_HALO, width), lambda bb, i, c: (bb, jnp.maximum(i * r - 1, 0), colblk0 + c)),
                  pl.BlockSpec((1, SSD_HALO, width),
                               lambda bb, i, c: (bb, jnp.minimum((i + 1) * r, nh - 1), colblk0 + c)),
                  pl.BlockSpec((V7X_SUBLANES, width), lambda bb, i, c: (0, c)),
                  pl.BlockSpec((1, width), lambda bb, i, c: (0, c))],
        out_specs=pl.BlockSpec((1, tt, width), lambda bb, i, c: (bb, i, c)),
        out_shape=jax.ShapeDtypeStruct((b, l, ncol * width), F32),
        scratch_shapes=[pltpu.VMEM((tt + 2 * SSD_HALO, width), F32)],
        compiler_params=_cparams(3, 8 * (tt + 2 * SSD_HALO) * width * 4),
        name="ssd_conv",
    )(u, u, u, wk, conv_b.reshape(1, -1).astype(F32))


def _ssd_kernel(*refs, reverse, finish, q, gw, nheads):
    if finish:
        (xbc_ref, dtr_ref, dtb_ref, arow_ref, sel_ref, s0_ref, yf_ref, z_ref, dvec_ref, nw_ref,
         y_ref, sfin_ref, st_s, yb_s) = refs
    else:
        xbc_ref, dtr_ref, dtb_ref, arow_ref, sel_ref, s0_ref, y_ref, sfin_ref, st_s = refs
        yb_s = y_ref.at[0]

    @pl.when(pl.program_id(1) == 0)
    def _():
        st_s[...] = s0_ref[0]

    n = SSD_STATE
    dt = _softplus(dtr_ref[0] + dtb_ref[...])
    dta = dt * arow_ref[...]
    row = lax.broadcasted_iota(jnp.int32, (q, q), 0)
    col = lax.broadcasted_iota(jnp.int32, (q, q), 1)
    tri = (row <= col) if reverse else (row >= col)
    tb = tri.astype(BF16)
    d1 = dta.astype(BF16)
    rem = dta - d1.astype(F32)
    d2 = rem.astype(BF16)
    d3 = (rem - d2.astype(F32)).astype(BF16)
    cum = jnp.dot(jnp.concatenate([tb, tb, tb], axis=1), jnp.concatenate([d1, d2, d3], axis=0),
                  preferred_element_type=F32)
    cum_t = cum.T
    dt_t = dt.T
    last = cum[0:1, :] if reverse else cum[q - 1:q, :]
    last_t = cum_t[:, 0:1] if reverse else cum_t[:, q - 1:q]
    wdt_t = jnp.exp(last_t - cum_t) * dt_t
    elast = jnp.exp(last)
    ecum = jnp.exp(cum)
    e1 = ecum.astype(BF16)
    r1 = ecum - e1.astype(F32)
    e2 = r1.astype(BF16)
    e3 = (r1 - e2.astype(F32)).astype(BF16)
    ecum_wide = jnp.dot(jnp.concatenate([e1, e2, e3], axis=1), sel_ref[...], preferred_element_type=F32)
    lo_q = lax.broadcasted_iota(jnp.int32, (q, V7X_LANES), 1) < SSD_HEAD_DIM
    lo_n = lax.broadcasted_iota(jnp.int32, (n, V7X_LANES), 1) < SSD_HEAD_DIM
    b_off = nheads * SSD_HEAD_DIM
    c_off = b_off + SSD_NGROUPS * n
    for g in range(SSD_NGROUPS):
        bg = xbc_ref[0, :, b_off + g * n:b_off + (g + 1) * n]
        bg_t = bg.T
        cb = xbc_ref[0, :, c_off + g * n:c_off + (g + 1) * n].astype(BF16)
        scores = lax.dot_general(cb, bg.astype(BF16), (((1,), (1,)), ((), ())), preferred_element_type=F32)
        for pr in range(SSD_HEADS_PER_GROUP // SSD_PAIR):
            pidx = g * (SSD_HEADS_PER_GROUP // SSD_PAIR) + pr
            lanes = slice(pidx * V7X_LANES, (pidx + 1) * V7X_LANES)
            xp = xbc_ref[0, :, lanes].astype(BF16)
            yd, stn = [], []
            for hh in range(SSD_PAIR):
                h = pidx * SSD_PAIR + hh
                seg = cum[:, h:h + 1] - cum_t[h:h + 1, :]
                dec = jnp.where(tri, jnp.exp(jnp.where(tri, seg, 0.0)), 0.0)
                w = (scores * dec * dt_t[h:h + 1, :]).astype(BF16)
                yd.append(jnp.dot(w, xp, preferred_element_type=F32))
                wb = (bg_t * wdt_t[h:h + 1, :]).astype(BF16)
                stn.append(jnp.dot(wb, xp, preferred_element_type=F32))
            h0 = pidx * SSD_PAIR
            s_prev = st_s[pidx]
            y_off = jnp.dot(cb, s_prev.astype(BF16), preferred_element_type=F32) * ecum_wide[:, lanes]
            st_s[pidx] = (s_prev * jnp.where(lo_n, elast[:, h0:h0 + 1], elast[:, h0 + 1:h0 + 2])
                          + jnp.where(lo_n, stn[0], stn[1]))
            yb_s[:, lanes] = jnp.where(lo_q, yd[0], yd[1]) + y_off
    sfin_ref[0] = st_s[...]
    if finish:
        xs = xbc_ref[0, :, :b_off]
        y = yf_ref[0] + yb_s[...] + dvec_ref[...] * xs
        t = y * _silu(z_ref[0])
        for g in range(SSD_NGROUPS):
            tg = t[:, g * gw:(g + 1) * gw]
            ms = jnp.mean(tg * tg, axis=-1, keepdims=True)
            y_ref[0, :, g * gw:(g + 1) * gw] = (tg * lax.rsqrt(ms + RMS_EPS)
                                                * nw_ref[:, g * gw:(g + 1) * gw]).astype(y_ref.dtype)


def ssd_scan(xbc, dtraw, dcol, dt_bias, a_head, s0, *, reverse, fin=None, q=128):
    b, l, cw = xbc.shape
    q = min(q, l)
    nc = l // q
    nheads = dt_bias.shape[0]
    width = nheads * SSD_HEAD_DIM
    npair = nheads // SSD_PAIR
    pad = lambda v: jnp.zeros((1, V7X_LANES), F32).at[0, :nheads].set(v.astype(F32))
    cmap = (lambda bb, c: (bb, nc - 1 - c, 0)) if reverse else (lambda bb, c: (bb, c, 0))
    dmap = (lambda bb, c: (bb, nc - 1 - c, dcol)) if reverse else (lambda bb, c: (bb, c, dcol))
    full = lambda shape: pl.BlockSpec(shape, lambda *_: (0,) * len(shape))
    st_spec = pl.BlockSpec((1, npair, SSD_STATE, V7X_LANES), lambda bb, c: (bb, 0, 0, 0))
    head_of_lane = np.arange(width) // SSD_HEAD_DIM
    sel = (np.arange(V7X_LANES)[:, None] == head_of_lane[None, :]).astype(np.float32)
    sel = jnp.asarray(np.concatenate([sel, sel, sel], axis=0), BF16)
    in_specs = [pl.BlockSpec((1, q, cw), cmap), pl.BlockSpec((1, q, V7X_LANES), dmap),
                full((1, V7X_LANES)), full((1, V7X_LANES)), full(sel.shape), st_spec]
    args = [xbc, dtraw, pad(dt_bias), pad(a_head), sel, s0]
    scratch = [pltpu.VMEM((npair, SSD_STATE, V7X_LANES), F32)]
    if fin is not None:
        y_other, z, zcol, dvec, nw = fin
        zmap = (lambda bb, c: (bb, nc - 1 - c, zcol)) if reverse else (lambda bb, c: (bb, c, zcol))
        in_specs += [pl.BlockSpec((1, q, width), cmap), pl.BlockSpec((1, q, width), zmap),
                     full((1, width)), full((1, width))]
        args += [y_other, z, dvec, nw]
        scratch.append(pltpu.VMEM((q, width), F32))
    out_dtype = BF16 if fin is not None else F32
    return pl.pallas_call(
        functools.partial(_ssd_kernel, reverse=reverse, finish=fin is not None, q=q,
                          gw=width // SSD_NGROUPS, nheads=nheads),
        grid=(b, nc),
        in_specs=in_specs,
        out_specs=[pl.BlockSpec((1, q, width), cmap), st_spec],
        out_shape=[jax.ShapeDtypeStruct((b, l, width), out_dtype),
                   jax.ShapeDtypeStruct((b, npair, SSD_STATE, V7X_LANES), F32)],
        scratch_shapes=scratch,
        compiler_params=_cparams(2, 4 * q * cw * 4 + 12 * q * width * 4 + 24 * q * q * 4),
        name="ssd_scan_bwd" if reverse else "ssd_scan_fwd",
    )(*args)


def ssd_mix(u, dtraw, zcol, xcol0, width, p, s0f, s0b):
    nheads = p["ssd_dt_bias"].shape[1]
    xbc = ssd_conv(u, xcol0, 2, width, p["ssd_conv_w"], p["ssd_conv_b"])
    a_head = -jnp.exp(p["ssd_a_log"].astype(F32))
    yf, sf = ssd_scan(xbc, dtraw, 0, p["ssd_dt_bias"][0], a_head[0], s0f, reverse=False)
    dvec = jnp.repeat(p["ssd_d"].astype(F32), SSD_HEAD_DIM).reshape(1, width)
    y, sb = ssd_scan(xbc, dtraw, 1, p["ssd_dt_bias"][1], a_head[1], s0b, reverse=True,
                     fin=(yf, u, zcol, dvec, p["ssd_norm_w"].reshape(1, width).astype(F32)))
    return y, sf, sb


def _cast_pad_kernel(x_ref, o_ref, *, nvalid):
    j = pl.program_id(1)

    @pl.when(j < nvalid)
    def _():
        o_ref[...] = x_ref[...].astype(o_ref.dtype)

    @pl.when(j >= nvalid)
    def _():
        o_ref[...] = jnp.zeros_like(o_ref)


def cast_pad_bf16(w, axis, tile, total):
    depth, r, c = w.shape
    size = w.shape[axis]
    assert size % tile == 0 and total % tile == 0
    nvalid = size // tile
    if axis == 2:
        block, out_shape = (1, r, tile), (depth, r, total)
        imap = lambda l, j: (l, 0, jnp.minimum(j, nvalid - 1))
        omap = lambda l, j: (l, 0, j)
    else:
        block, out_shape = (1, tile, c), (depth, total, c)
        imap = lambda l, j: (l, jnp.minimum(j, nvalid - 1), 0)
        omap = lambda l, j: (l, j, 0)
    nelem = block[1] * block[2]
    return pl.pallas_call(
        functools.partial(_cast_pad_kernel, nvalid=nvalid),
        grid=(depth, total // tile),
        in_specs=[pl.BlockSpec(block, imap)],
        out_specs=pl.BlockSpec(block, omap),
        out_shape=jax.ShapeDtypeStruct(out_shape, BF16),
        compiler_params=_cparams(2, 2 * nelem * 6),
        name="cast_pad_bf16",
    )(w)


def _dt_weights(w_in, n_main, nheads):
    pad_last = lambda w: jnp.pad(w, ((0, 0), (0, 0), (0, V7X_LANES - w.shape[2])))
    return jnp.concatenate([pad_last(w_in[:, :, n_main:n_main + nheads]),
                            pad_last(w_in[:, :, n_main + nheads:])], axis=2)


def _mixers(hm, wts, p, s5_h0, ssd_s0, want_output=True):
    b, l, d = hm.shape
    gw = d // N_MIXERS
    hm2 = hm.reshape(b * l, d)
    u = matmul(hm2, wts["w_in"], layer=wts["layer"], n_out=wts["n_main"], tn=1024,
               name="w_in_main").reshape(b, l, -1)
    dtraw = matmul(hm2, wts["w_dt"], layer=wts["layer"], tn=2 * V7X_LANES, name="w_in_dt").reshape(b, l, -1)
    ys, s5f, s5b = s5_mix(u, 1, gw, p, *s5_h0)
    yd, sdf, sdb = ssd_mix(u, dtraw, 4, 5, gw, p, *ssd_s0)
    states = ((s5f, s5b), (sdf, sdb))
    if not want_output:
        return None, states
    ya = fnet_mix(u, 0, gw)
    yc = conv_module(u, 2, 3, gw, p)
    return [y.reshape(b * l, gw) for y in (ya, ys, yc, yd)], states


def _layer_tail(h, mix, wts, mod, mod_next, ln, *, alpha, rows_per_batch):
    (ln1_g, ln1_b, ln2_g, ln2_b) = ln
    layer = wts["layer"]
    out = matmul(mix, wts["w_out"], layer=layer, tm=512, tn=1024, name="w_out")
    h, hm = residual_layernorm(h, out, mod[2], ln1_g, ln1_b, mod[4], mod[3], alpha=alpha,
                               rows_per_batch=rows_per_batch)
    hid = matmul(hm, [wts["w_gate"], wts["w_up"]], layer=layer, tm=1024, tn=FFN_TILE, out_dtype=BF16,
                 name="gate_up")
    ff = matmul(hid, wts["w_down"], layer=layer, tm=512, tn=512, name="w_down")
    return residual_layernorm(h, ff, mod[5], ln2_g, ln2_b, mod_next[1], mod_next[0], alpha=alpha,
                              rows_per_batch=rows_per_batch)


def kernel(x, c, ctx, c_ctx, w_ada, b_ada, w_in, s5_lam_re, s5_lam_im, s5_log_dt, s5_b_re, s5_b_im,
           s5_c_re, s5_c_im, s5_d, s5_w_glu, cv_w, cv_b, cv_ln_g, cv_ln_b, ssd_conv_w, ssd_conv_b,
           ssd_a_log, ssd_dt_bias, ssd_d, ssd_norm_w, w_out, ln1_g, ln1_b, w_gate, w_up, w_down,
           ln2_g, ln2_b):
    bsz, seq, d = x.shape
    cl = ctx.shape[1]
    depth = w_in.shape[0]
    gw = d // N_MIXERS
    alpha = (2 * depth) ** 0.25
    per_layer = dict(s5_lam_re=s5_lam_re, s5_lam_im=s5_lam_im, s5_log_dt=s5_log_dt,
                     s5_b_re=s5_b_re, s5_b_im=s5_b_im, s5_c_re=s5_c_re, s5_c_im=s5_c_im, s5_d=s5_d,
                     s5_w_glu=s5_w_glu, cv_w=cv_w, cv_b=cv_b, cv_ln_g=cv_ln_g, cv_ln_b=cv_ln_b,
                     ssd_conv_w=ssd_conv_w, ssd_conv_b=ssd_conv_b, ssd_a_log=ssd_a_log,
                     ssd_dt_bias=ssd_dt_bias, ssd_d=ssd_d, ssd_norm_w=ssd_norm_w)

    c8 = jnp.zeros((V7X_SUBLANES, d), F32).at[:bsz].set(c).at[bsz].set(c_ctx)
    mods = ada_modulation(c8, w_ada, b_ada)

    def mod_vectors(i):
        m = mods[i].reshape(V7X_SUBLANES, 6, d)
        lat = [m[:bsz, k].reshape(bsz, 1, d) for k in range(6)]
        cx = [jnp.broadcast_to(m[bsz, k].reshape(1, 1, d), (bsz, 1, d)) for k in range(6)]
        return lat, cx

    nstate = (gw // V7X_LANES) * S5_STATE_BLOCK
    npair = ssd_dt_bias.shape[2] // SSD_PAIR
    z_s5 = (jnp.zeros((bsz, 1, nstate), F32), jnp.zeros((bsz, 1, nstate), F32))
    z_ssd = jnp.zeros((bsz, npair, SSD_STATE, V7X_LANES), F32)

    n_main = 4 * gw + gw + (gw + 2 * SSD_NGROUPS * SSD_STATE)
    w_dt = _dt_weights(w_in, n_main, ssd_dt_bias.shape[2])
    hidden = w_gate.shape[2]
    hidden_pad = -(-hidden // FFN_TILE) * FFN_TILE
    cast_tile = math.gcd(hidden, FFN_TILE)
    w_gate_bf = cast_pad_bf16(w_gate, 2, cast_tile, hidden_pad)
    w_up_bf = cast_pad_bf16(w_up, 2, cast_tile, hidden_pad)
    w_down_bf = cast_pad_bf16(w_down, 1, cast_tile, hidden_pad)
    lat0, cx0 = mod_vectors(0)
    h, hm = prologue(x, lat0[1], lat0[0])
    h = h.reshape(bsz * seq, d)
    hc = ctx.reshape(bsz * cl, d)
    hcm = modulate(ctx, cx0[1], cx0[0])
    for i in range(depth):
        with_ctx = i < depth - 1
        p = {k: v[i] for k, v in per_layer.items()}
        wts = dict(layer=i, n_main=n_main, w_in=w_in, w_dt=w_dt, w_out=w_out, w_gate=w_gate_bf,
                   w_up=w_up_bf, w_down=w_down_bf)
        lat, cx = mod_vectors(i)
        lat_next, cx_next = mod_vectors(min(i + 1, depth - 1))
        ln = (ln1_g[i], ln1_b[i], ln2_g[i], ln2_b[i])
        mix_c, ((c5f, c5b), (cdf, cdb)) = _mixers(hcm.reshape(bsz, cl, d), wts, p, (z_s5, z_s5),
                                                   (z_ssd, z_ssd), want_output=with_ctx)
        mix_l, _ = _mixers(hm.reshape(bsz, seq, d), wts, p, (c5f, c5b), (cdf, cdb))
        h, hm = _layer_tail(h, mix_l, wts, lat, lat_next, ln, alpha=alpha, rows_per_batch=seq)
        if with_ctx:
            hc, hcm = _layer_tail(hc, mix_c, wts, cx, cx_next, ln, alpha=alpha, rows_per_batch=cl)
    return h.reshape(bsz, seq, d).astype(x.dtype)
```
